```python
import functools
import jax, jax.numpy as jnp
from jax import lax
import numpy as np

D_MODEL = 1024
BATCH = 8
SEQ = 4096
DEPTH = 2
DEC_BATCH = 128
DEC_SEQ = 1
PAST_LEN = 16384
PAGE_SIZE = 128

D_LRU = 512
LRU_BLOCKS = 8
LRU_BLOCK = D_LRU // LRU_BLOCKS
CONV_W = 4
LRU_C = 8.0
N_HEADS = 8
D_NOPE = 64
D_ROPE = 32
D_QK = D_NOPE + D_ROPE
D_V = 64
Q_RANK = 384
KV_RANK = 256
ROPE_THETA = 10000.0
D_ATT = N_HEADS * D_V
D_MIX = D_LRU + D_ATT
D_IN = 2 * D_LRU + Q_RANK + KV_RANK + D_ROPE
IN_SPLITS = (D_LRU, 2 * D_LRU, 2 * D_LRU + Q_RANK, 2 * D_LRU + Q_RANK + KV_RANK)
D_FF = 4 * D_MODEL
Q_BLOCK = 128
ALPHA = (2 * DEPTH) ** 0.25
BETA = (8 * DEPTH) ** -0.25
ATT_SCALE = D_QK ** -0.5
EPS = 1e-6

kernel_name = 'hymba_rglru_mla_deepnorm_adaln_step'


def layer_norm(x, g, b):
    xf = x.astype(jnp.float32)
    mu = jnp.mean(xf, axis=-1, keepdims=True)
    var = jnp.mean(jnp.square(xf - mu), axis=-1, keepdims=True)
    return ((xf - mu) * lax.rsqrt(var + EPS)).astype(x.dtype) * g + b


def rms_norm(x, g):
    xf = x.astype(jnp.float32)
    return (xf * lax.rsqrt(jnp.mean(xf * xf, axis=-1, keepdims=True) + EPS)).astype(x.dtype) * g


def rope(x, pos):
    half = D_ROPE // 2
    inv = ROPE_THETA ** (-jnp.arange(half, dtype=jnp.float32) / half)
    ang = pos.astype(jnp.float32)[:, None] * inv[None, :]
    cos = jnp.cos(ang)[None, :, None, :]
    sin = jnp.sin(ang)[None, :, None, :]
    x1 = x[..., :half].astype(jnp.float32)
    x2 = x[..., half:].astype(jnp.float32)
    return jnp.concatenate([x1 * cos - x2 * sin, x2 * cos + x1 * sin], axis=-1).astype(x.dtype)


def ada_mod(c, w_ada, b_ada):
    m = jnp.einsum('bd,de->be', jax.nn.silu(c), w_ada) + b_ada
    return [t[:, None, :] for t in jnp.split(m, 6, axis=-1)]


def causal_conv(u, buf, conv_w, conv_b):
    T = u.shape[1]
    up = jnp.concatenate([buf.astype(u.dtype), u], axis=1)
    out = conv_b
    for k in range(CONV_W):
        out = out + up[:, k:k + T] * conv_w[k]
    return out, up[:, T:]


def rg_lru(u, h0, w_a, b_a, w_x, b_x, lru_lambda):
    B, T, _ = u.shape
    ub = u.reshape(B, T, LRU_BLOCKS, LRU_BLOCK)
    r = jax.nn.sigmoid(jnp.einsum('btni,nij->btnj', ub, w_a).reshape(B, T, D_LRU) + b_a)
    i = jax.nn.sigmoid(jnp.einsum('btni,nij->btnj', ub, w_x).reshape(B, T, D_LRU) + b_x)
    log_a = (-LRU_C * r * jax.nn.softplus(-lru_lambda)).astype(jnp.float32)
    a = jnp.exp(log_a)
    b = jnp.sqrt(-jnp.expm1(2.0 * log_a)) * (i * u).astype(jnp.float32)

    def step(h, ab):
        h = ab[0] * h + ab[1]
        return h, h

    h_last, hs = lax.scan(step, h0.astype(jnp.float32), (jnp.swapaxes(a, 0, 1), jnp.swapaxes(b, 0, 1)))
    return jnp.swapaxes(hs, 0, 1).astype(u.dtype), h_last.astype(h0.dtype)


def attend_prompt(q_nope, q_rope, ckv, kr, w_uk, w_uv):
    B, S = ckv.shape[0], ckv.shape[1]
    k_nope = jnp.einsum('bsr,rhd->bshd', ckv, w_uk)
    v = jnp.einsum('bsr,rhv->bshv', ckv, w_uv)
    nb = S // Q_BLOCK
    qn = jnp.swapaxes(q_nope.reshape(B, nb, Q_BLOCK, N_HEADS, D_NOPE), 0, 1)
    qr = jnp.swapaxes(q_rope.reshape(B, nb, Q_BLOCK, N_HEADS, D_ROPE), 0, 1)
    key_pos = jnp.arange(S, dtype=jnp.int32)

    def one_block(args):
        qn_b, qr_b, blk = args
        s = jnp.einsum('bqhd,bshd->bhqs', qn_b, k_nope) + jnp.einsum('bqhe,bse->bhqs', qr_b, kr)
        s = s.astype(jnp.float32) * ATT_SCALE
        q_pos = blk * Q_BLOCK + jnp.arange(Q_BLOCK, dtype=jnp.int32)
        s = jnp.where(key_pos[None, :] <= q_pos[:, None], s, -jnp.inf)
        p = jax.nn.softmax(s, axis=-1).astype(v.dtype)
        return jnp.einsum('bhqs,bshv->bqhv', p, v)

    o = lax.map(one_block, (qn, qr, jnp.arange(nb, dtype=jnp.int32)))
    return jnp.swapaxes(o, 0, 1).reshape(B, S, N_HEADS, D_V)


def attend_sample(q_nope, q_rope, ckv, kr, w_uk, w_uv, past_ckv, past_kr):
    P = past_ckv.shape[1]
    T = ckv.shape[1]
    q_lat = jnp.einsum('bthd,rhd->bthr', q_nope, w_uk)
    s_past = jnp.einsum('bthr,bsr->bhts', q_lat, past_ckv) + jnp.einsum('bthe,bse->bhts', q_rope, past_kr)
    s_new = jnp.einsum('bthr,bsr->bhts', q_lat, ckv) + jnp.einsum('bthe,bse->bhts', q_rope, kr)
    t_idx = jnp.arange(T, dtype=jnp.int32)
    s_new = jnp.where(t_idx[None, :] <= t_idx[:, None], s_new.astype(jnp.float32), -jnp.inf)
    s = jnp.concatenate([s_past.astype(jnp.float32), s_new], axis=-1) * ATT_SCALE
    p = jax.nn.softmax(s, axis=-1).astype(ckv.dtype)
    o_lat = jnp.einsum('bhts,bsr->bthr', p[..., :P], past_ckv) + jnp.einsum('bhts,bsr->bthr', p[..., P:], ckv)
    return jnp.einsum('bthr,rhv->bthv', o_lat, w_uv)


def trunk_layer(x, c, pos, h0, buf, attend, w_ada, b_ada, w_in, conv_w, conv_b, w_a, b_a, w_x, b_x,
                lru_lambda, q_norm_g, w_uq, kv_norm_g, w_uk, w_uv, g_lru, g_att, w_o, ln_g, ln_b,
                w_up, w_down):
    B, T, _ = x.shape
    sh_m, sc_m, gt_m, sh_f, sc_f, gt_f = ada_mod(c, w_ada, b_ada)
    h = x * (1.0 + sc_m) + sh_m
    proj = jnp.einsum('btd,de->bte', h, w_in)
    u, gate, cq, ckv, kr = jnp.split(proj, IN_SPLITS, axis=-1)
    u, new_buf = causal_conv(u, buf, conv_w, conv_b)
    y_lru, new_h = rg_lru(u, h0, w_a, b_a, w_x, b_x, lru_lambda)
    y_lru = y_lru * jax.nn.gelu(gate)
    q = jnp.einsum('btr,rhd->bthd', rms_norm(cq, q_norm_g), w_uq)
    q_nope = q[..., :D_NOPE]
    q_rope = rope(q[..., D_NOPE:], pos)
    ckv = rms_norm(ckv, kv_norm_g)
    kr = rope(kr[:, :, None, :], pos)[:, :, 0]
    y_att = attend(q_nope, q_rope, ckv, kr, w_uk, w_uv).reshape(B, T, D_ATT)
    mix = jnp.concatenate([rms_norm(y_lru, g_lru), rms_norm(y_att, g_att)], axis=-1)
    x = layer_norm(ALPHA * x + gt_m * jnp.einsum('bte,ed->btd', mix, w_o), ln_g[0], ln_b[0])
    h = x * (1.0 + sc_f) + sh_f
    f = jnp.einsum('btf,fd->btd', jnp.square(jax.nn.relu(jnp.einsum('btd,df->btf', h, w_up))), w_down)
    x = layer_norm(ALPHA * x + gt_f * f, ln_g[1], ln_b[1])
    return x, ckv, kr, new_h, new_buf


def setup_inputs(seed: int = 0) -> dict:
    key = jax.random.key(seed)
    ks = iter(jax.random.split(key, 48))

    def nrm(shape, s):
        return jax.random.normal(next(ks), shape, jnp.float32) * s

    n_pages = PAST_LEN // PAGE_SIZE
    n_pool = (DEC_BATCH * n_pages * 5) // 4
    perm = jax.random.permutation(next(ks), n_pool)
    page_table = perm[:DEC_BATCH * n_pages].reshape(DEC_BATCH, n_pages).astype(jnp.int32)
    a0 = jax.random.uniform(next(ks), (DEPTH, D_LRU), jnp.float32, 0.9, 0.999)
    sa = a0 ** (1.0 / LRU_C)
    lru_lambda = jnp.log(sa) - jnp.log1p(-sa)
    return {
        'x_prompt': nrm((BATCH, SEQ, D_MODEL), 1.0),
        'x_sample': nrm((DEC_BATCH, DEC_SEQ, D_MODEL), 1.0),
        'cache_ckv': nrm((DEPTH, n_pool, PAGE_SIZE, KV_RANK), 1.0),
        'cache_krope': nrm((DEPTH, n_pool, PAGE_SIZE, D_ROPE), 1.0),
        'state_lru_h': nrm((DEPTH, DEC_BATCH, D_LRU), 0.5),
        'state_conv': nrm((DEPTH, DEC_BATCH, CONV_W - 1, D_LRU), 1.0),
        'page_table': page_table,
        'c_prompt': nrm((BATCH, D_MODEL), 1.0),
        'c_sample': nrm((DEC_BATCH, D_MODEL), 1.0),
        'ln0_g': 1.0 + nrm((D_MODEL,), 0.02),
        'ln0_b': nrm((D_MODEL,), 0.02),
        'w_ada': nrm((DEPTH, D_MODEL, 6 * D_MODEL), D_MODEL ** -0.5),
        'b_ada': nrm((DEPTH, 6 * D_MODEL), 0.02),
        'w_in': nrm((DEPTH, D_MODEL, D_IN), D_MODEL ** -0.5),
        'conv_w': nrm((DEPTH, CONV_W, D_LRU), CONV_W ** -0.5),
        'conv_b': nrm((DEPTH, D_LRU), 0.02),
        'w_a': nrm((DEPTH, LRU_BLOCKS, LRU_BLOCK, LRU_BLOCK), LRU_BLOCK ** -0.5),
        'b_a': nrm((DEPTH, D_LRU), 0.02),
        'w_x': nrm((DEPTH, LRU_BLOCKS, LRU_BLOCK, LRU_BLOCK), LRU_BLOCK ** -0.5),
        'b_x': nrm((DEPTH, D_LRU), 0.02),
        'lru_lambda': lru_lambda,
        'q_norm_g': 1.0 + nrm((DEPTH, Q_RANK), 0.02),
        'w_uq': nrm((DEPTH, Q_RANK, N_HEADS, D_QK), Q_RANK ** -0.5),
        'kv_norm_g': 1.0 + nrm((DEPTH, KV_RANK), 0.02),
        'w_uk': nrm((DEPTH, KV_RANK, N_HEADS, D_NOPE), KV_RANK ** -0.5),
        'w_uv': nrm((DEPTH, KV_RANK, N_HEADS, D_V), KV_RANK ** -0.5),
        'g_lru': 1.0 + nrm((DEPTH, D_LRU), 0.02),
        'g_att': 1.0 + nrm((DEPTH, D_ATT), 0.02),
        'w_o': nrm((DEPTH, D_MIX, D_MODEL), BETA * D_MIX ** -0.5),
        'ln_g': 1.0 + nrm((DEPTH, 2, D_MODEL), 0.02),
        'ln_b': nrm((DEPTH, 2, D_MODEL), 0.02),
        'w_up': nrm((DEPTH, D_MODEL, D_FF), D_MODEL ** -0.5),
        'w_down': nrm((DEPTH, D_FF, D_MODEL), BETA * D_FF ** -0.5),
    }


def reference(x_prompt, x_sample, cache_ckv, cache_krope, state_lru_h, state_conv, page_table,
              c_prompt, c_sample, ln0_g, ln0_b, w_ada, b_ada, w_in, conv_w, conv_b, w_a, b_a,
              w_x, b_x, lru_lambda, q_norm_g, w_uq, kv_norm_g, w_uk, w_uv, g_lru, g_att, w_o,
              ln_g, ln_b, w_up, w_down):
    b_p, s_p = x_prompt.shape[0], x_prompt.shape[1]
    b_s, s_s = x_sample.shape[0], x_sample.shape[1]
    past_len = page_table.shape[1] * cache_ckv.shape[2]
    pos_p = jnp.arange(s_p, dtype=jnp.int32)
    pos_s = past_len + jnp.arange(s_s, dtype=jnp.int32)
    xp = layer_norm(x_prompt, ln0_g, ln0_b)
    xs = layer_norm(x_sample, ln0_g, ln0_b)
    ckv_p, kr_p, h_p, cv_p = [], [], [], []
    ckv_s, kr_s, h_s, cv_s = [], [], [], []
    for l in range(DEPTH):
        lw = (w_ada[l], b_ada[l], w_in[l], conv_w[l], conv_b[l], w_a[l], b_a[l], w_x[l], b_x[l],
              lru_lambda[l], q_norm_g[l], w_uq[l], kv_norm_g[l], w_uk[l], w_uv[l], g_lru[l],
              g_att[l], w_o[l], ln_g[l], ln_b[l], w_up[l], w_down[l])
        h0 = jnp.zeros((b_p, D_LRU), x_prompt.dtype)
        buf0 = jnp.zeros((b_p, CONV_W - 1, D_LRU), x_prompt.dtype)
        xp, a1, a2, a3, a4 = trunk_layer(xp, c_prompt, pos_p, h0, buf0, attend_prompt, *lw)
        ckv_p.append(a1); kr_p.append(a2); h_p.append(a3); cv_p.append(a4)
        past_ckv = cache_ckv[l, page_table].reshape(b_s, past_len, KV_RANK)
        past_kr = cache_krope[l, page_table].reshape(b_s, past_len, D_ROPE)
        attend = functools.partial(attend_sample, past_ckv=past_ckv, past_kr=past_kr)
        xs, s1, s2, s3, s4 = trunk_layer(xs, c_sample, pos_s, state_lru_h[l], state_conv[l], attend, *lw)
        ckv_s.append(s1); kr_s.append(s2); h_s.append(s3); cv_s.append(s4)
    new_ckv_p = jnp.stack(ckv_p, 0)
    new_kr_p = jnp.stack(kr_p, 0)
    new_h_p = jnp.stack(h_p, 0)
    new_cv_p = jnp.stack(cv_p, 0)
    new_ckv_s = jnp.stack(ckv_s, 0)
    new_kr_s = jnp.stack(kr_s, 0)
    new_h_s = jnp.stack(h_s, 0)
    new_cv_s = jnp.stack(cv_s, 0)
    return (xp, xs, new_ckv_p, new_kr_p, new_h_p, new_cv_p, new_ckv_s, new_kr_s, new_h_s, new_cv_s)
```

```python
import functools
import math

import jax
import jax.numpy as jnp
import numpy as np
from jax import lax
from jax.experimental import pallas as pl
from jax.experimental.pallas import tpu as pltpu

D_MODEL = 1024
D_LRU = 512
LRU_BLOCKS = 8
LRU_BLOCK = D_LRU // LRU_BLOCKS
CONV_W = 4
LRU_C = 8.0
N_HEADS = 8
D_NOPE = 64
D_ROPE = 32
D_QK = D_NOPE + D_ROPE
D_V = 64
Q_RANK = 384
KV_RANK = 256
ROPE_THETA = 10000.0
D_FF = 4 * D_MODEL
DEPTH = 2
ALPHA = (2 * DEPTH) ** 0.25
ATT_SCALE = D_QK ** -0.5
EPS = 1e-6

LANES = 128
SUBLANES = 8
VMEM_BYTES_V7X = 64 * 1024 * 1024
VMEM_LIMIT = VMEM_BYTES_V7X - 8 * 1024 * 1024

HEAD_PAD = LANES
D_HEADS_PAD = N_HEADS * HEAD_PAD
HALF_ROPE = D_ROPE // 2
OFF_U, OFF_GATE, OFF_CQ, OFF_CKV = 0, D_LRU, 2 * D_LRU, 2 * D_LRU + Q_RANK
OFF_KRA = OFF_CKV + KV_RANK
OFF_KRB = OFF_KRA + LANES
D_IN_PAD = OFF_KRB + LANES
EXP2_SCALE = ATT_SCALE * math.log2(math.e)

F32 = jnp.float32
BF16 = jnp.bfloat16


def _prompt_tile(seq):
    for t in (512, 256, 128, 64, 32, 16, 8):
        if seq % t == 0:
            return t
    raise ValueError(f"sequence length {seq} must be a multiple of 8")


def _const_spec(shape):
    nd = len(shape)
    return pl.BlockSpec(shape, lambda *_: (0,) * nd, pipeline_mode=pl.Buffered(1))


def _layer_norm(x, g, b):
    mu = jnp.mean(x, axis=-1, keepdims=True)
    xc = x - mu
    var = jnp.mean(xc * xc, axis=-1, keepdims=True)
    return xc * lax.rsqrt(var + EPS) * g + b


def _rms_norm(x, g, n):
    ms = jnp.sum(x * x, axis=-1, keepdims=True) * (1.0 / n)
    return x * lax.rsqrt(ms + EPS) * g


def _sigmoid(x):
    return 1.0 / (1.0 + jnp.exp(-x))


def _softplus(x):
    return jnp.maximum(x, 0.0) + jnp.log1p(jnp.exp(-jnp.abs(x)))


def _gelu_tanh(x):
    c = math.sqrt(2.0 / math.pi)
    return 0.5 * x * (1.0 + jnp.tanh(c * (x + 0.044715 * (x * x * x))))


def _bdot(a, b):
    return jnp.dot(a, b, preferred_element_type=F32)


def _lru_coeffs(uc, wg_ref, ba_ref, bx_ref, lam_ref):
    g = _bdot(uc.astype(BF16), wg_ref[...])
    r = _sigmoid(g[:, :D_LRU] + ba_ref[...])
    i = _sigmoid(g[:, D_LRU:] + bx_ref[...])
    log_a = (-LRU_C) * r * _softplus(-lam_ref[...])
    a = jnp.exp(log_a)
    t = jnp.tanh(log_a)
    b = jnp.sqrt((-2.0 * t) / (1.0 - t)) * (i * uc)
    return a, b


def _latent_heads(proj, cq_tab, sq_tab, ck_tab, sk_tab, qg_ref, wuq_ref, kvg_ref):
    cqn = _rms_norm(proj[:, OFF_CQ:OFF_CKV], qg_ref[...], Q_RANK)
    qq = _bdot(cqn.astype(BF16), wuq_ref[...])
    q = qq[:, :D_HEADS_PAD] * cq_tab + qq[:, D_HEADS_PAD:] * sq_tab
    ckvn = _rms_norm(proj[:, OFF_CKV:OFF_KRA], kvg_ref[...], KV_RANK)
    krw = proj[:, OFF_KRA:OFF_KRB] * ck_tab + proj[:, OFF_KRB:D_IN_PAD] * sk_tab
    return q, ckvn, krw


def _ada_kernel(c_ref, w_ref, b_ref, o_ref):
    c = c_ref[...]
    s = (c * _sigmoid(c)).astype(BF16)
    o_ref[0] = _bdot(s, w_ref[0].astype(BF16)) + b_ref[0]


def _ada_mod(c_all, w_ada, b_ada):
    rows = c_all.shape[0]
    depth, _, n_out = w_ada.shape
    tn = 1536
    return pl.pallas_call(
        _ada_kernel,
        grid=(depth, n_out // tn),
        in_specs=[
            pl.BlockSpec((rows, D_MODEL), lambda l, j: (0, 0)),
            pl.BlockSpec((1, D_MODEL, tn), lambda l, j: (l, 0, j)),
            pl.BlockSpec((1, 1, tn), lambda l, j: (l, 0, j)),
        ],
        out_specs=pl.BlockSpec((1, rows, tn), lambda l, j: (l, 0, j)),
        out_shape=jax.ShapeDtypeStruct((depth, rows, n_out), F32),
        compiler_params=pltpu.CompilerParams(
            dimension_semantics=("arbitrary", "arbitrary"), vmem_limit_bytes=VMEM_LIMIT),
        name="ada_mod",
    )(c_all, w_ada, b_ada.reshape(depth, 1, n_out))


def _rope_table_kernel(pos0, inv_ref, one_ref, rope_ref, cos_ref, sin_ref):
    rows = cos_ref.shape[0]
    pos = (lax.broadcasted_iota(jnp.int32, (rows, 1), 0)
           + (pl.program_id(0) * rows + pos0)).astype(F32)
    ang = pos * inv_ref[...]
    cos_ref[...] = rope_ref[...] * jnp.cos(ang) + one_ref[...]
    sin_ref[...] = rope_ref[...] * jnp.sin(ang)


def _rope_tables(rows, pos0, inv_row, one_row, rope_row):
    width = inv_row.shape[1]
    tr = _prompt_tile(rows)
    return pl.pallas_call(
        functools.partial(_rope_table_kernel, pos0),
        grid=(rows // tr,),
        in_specs=[_const_spec((1, width))] * 3,
        out_specs=[pl.BlockSpec((tr, width), lambda i: (i, 0))] * 2,
        out_shape=[jax.ShapeDtypeStruct((rows, width), F32)] * 2,
        compiler_params=pltpu.CompilerParams(dimension_semantics=("arbitrary",)),
        name="rope_tables",
    )(inv_row, one_row, rope_row)


def _rope_lane_rows():
    inv = ROPE_THETA ** (-np.arange(HALF_ROPE, dtype=np.float64) / HALF_ROPE)
    head_inv = np.zeros(HEAD_PAD)
    head_inv[D_NOPE:D_NOPE + HALF_ROPE] = inv
    head_inv[D_NOPE + HALF_ROPE:D_QK] = inv
    head_rope = (head_inv > 0).astype(np.float64)
    head_one = np.zeros(HEAD_PAD)
    head_one[:D_NOPE] = 1.0
    key_inv = np.zeros(LANES)
    key_inv[:HALF_ROPE] = inv
    key_inv[HALF_ROPE:D_ROPE] = inv
    key_rope = (key_inv > 0).astype(np.float64)
    row = lambda v, reps: jnp.asarray(np.tile(v, reps)[None, :], F32)
    q_rows = (row(head_inv, N_HEADS), row(head_one, N_HEADS), row(head_rope, N_HEADS))
    k_rows = (row(key_inv, 1), row(np.zeros(LANES), 1), row(key_rope, 1))
    return q_rows, k_rows


def _prompt_pre_kernel(apply_ln0, tm, *refs):
    (x_ref, mod_ref, ln0g_ref, ln0b_ref, win_ref, convw_ref, convb_ref, wg_ref, ba_ref,
     bx_ref, lam_ref, qg_ref, wuq_ref, kvg_ref, wkv_ref, glru_ref,
     cq_ref, sq_ref, ck_ref, sk_ref) = refs[:20]
    outs = refs[20:]
    if apply_ln0:
        xn_ref, outs = outs[0], outs[1:]
    (q_ref, k_ref, v_ref, mixl_ref, ckv_ref, kr_ref, hlast_ref, conv_ref,
     hcar_ref, halo_ref, ubuf_ref, a_sc, b_sc, hs_sc) = outs

    s_idx = pl.program_id(0)
    b_idx = pl.program_id(1)

    x = x_ref[0]
    if apply_ln0:
        x = _layer_norm(x, ln0g_ref[...], ln0b_ref[...])
        xn_ref[0] = x
    h = (x * (1.0 + mod_ref[1, 0]) + mod_ref[0, 0]).astype(BF16)
    proj = _bdot(h, win_ref[...])
    u_raw = proj[:, OFF_U:OFF_GATE]
    gate = proj[:, OFF_GATE:OFF_CQ]

    @pl.when(s_idx == 0)
    def _():
        halo_ref[b_idx] = jnp.zeros((SUBLANES, D_LRU), F32)
        hcar_ref[b_idx] = jnp.zeros((1, D_LRU), F32)

    ubuf_ref[0:SUBLANES, :] = halo_ref[b_idx]
    ubuf_ref[SUBLANES:, :] = u_raw
    halo_ref[b_idx] = u_raw[tm - SUBLANES:, :]
    uc = convb_ref[...] + u_raw * convw_ref[CONV_W - 1:CONV_W, :]
    for kk in range(CONV_W - 1):
        off = SUBLANES - (CONV_W - 1) + kk
        uc = uc + ubuf_ref[off:off + tm, :] * convw_ref[kk:kk + 1, :]
    conv_ref[0, 0] = ubuf_ref[tm + SUBLANES - (CONV_W - 1):tm + SUBLANES, :]

    a, bb = _lru_coeffs(uc, wg_ref, ba_ref, bx_ref, lam_ref)

    row8 = lax.broadcasted_iota(jnp.int32, (tm, D_LRU), 0) & (SUBLANES - 1)
    for d in (1, 2, 4):
        a_sh = pltpu.roll(a, d, axis=0)
        b_sh = pltpu.roll(bb, d, axis=0)
        take = row8 >= d
        bb = jnp.where(take, a * b_sh + bb, bb)
        a = jnp.where(take, a * a_sh, a)
    a_sc[...] = a
    b_sc[...] = bb

    def group_step(g, hprev):
        off = pl.multiple_of(g * SUBLANES, SUBLANES)
        hg = a_sc[pl.ds(off, SUBLANES), :] * hprev + b_sc[pl.ds(off, SUBLANES), :]
        hs_sc[pl.ds(off, SUBLANES), :] = hg
        return hg[SUBLANES - 1:SUBLANES, :]

    h_fin = lax.fori_loop(0, tm // SUBLANES, group_step, hcar_ref[b_idx], unroll=8)
    hcar_ref[b_idx] = h_fin
    hlast_ref[0, 0] = h_fin

    y = hs_sc[...] * _gelu_tanh(gate)
    mixl_ref[0] = _rms_norm(y, glru_ref[...], D_LRU).astype(BF16)

    q, ckvn, krw = _latent_heads(proj, cq_ref[...], sq_ref[...], ck_ref[...], sk_ref[...],
                                 qg_ref, wuq_ref, kvg_ref)
    q_ref[0] = q.astype(BF16)
    ckv_ref[0] = ckvn
    kr_ref[0] = krw[:, :D_ROPE]
    kv = _bdot(jnp.concatenate([ckvn, krw], axis=1).astype(BF16), wkv_ref[...])
    k_ref[0] = kv[:, :D_HEADS_PAD].astype(BF16)
    v_ref[0] = kv[:, D_HEADS_PAD:].astype(BF16)


def _prompt_pre(apply_ln0, x, mod, ln0_g, ln0_b, lw, tabs):
    bsz, seq, _ = x.shape
    tm = _prompt_tile(seq)
    ns = seq // tm
    cq_tab, sq_tab, ck_tab, sk_tab = tabs
    tile = lambda w: pl.BlockSpec((1, tm, w), lambda s, b: (b, s, 0))
    in_specs = [
        tile(D_MODEL),
        pl.BlockSpec((6, 1, 1, D_MODEL), lambda s, b: (0, b, 0, 0)),
        _const_spec((1, D_MODEL)), _const_spec((1, D_MODEL)),
        _const_spec(lw["w_in"].shape), _const_spec((CONV_W, D_LRU)), _const_spec((1, D_LRU)),
        _const_spec(lw["w_gates"].shape), _const_spec((1, D_LRU)), _const_spec((1, D_LRU)),
        _const_spec((1, D_LRU)), _const_spec((1, Q_RANK)), _const_spec(lw["w_uq"].shape),
        _const_spec((1, KV_RANK)), _const_spec(lw["w_kv"].shape), _const_spec((1, D_LRU)),
        pl.BlockSpec((tm, D_HEADS_PAD), lambda s, b: (s, 0)),
        pl.BlockSpec((tm, D_HEADS_PAD), lambda s, b: (s, 0)),
        pl.BlockSpec((tm, LANES), lambda s, b: (s, 0)),
        pl.BlockSpec((tm, LANES), lambda s, b: (s, 0)),
    ]
    out_specs = [
        tile(D_HEADS_PAD), tile(D_HEADS_PAD), tile(D_HEADS_PAD), tile(D_LRU), tile(KV_RANK),
        tile(D_ROPE),
        pl.BlockSpec((1, 1, 1, D_LRU), lambda s, b: (s, b, 0, 0)),
        pl.BlockSpec((1, 1, CONV_W - 1, D_LRU), lambda s, b: (s, b, 0, 0)),
    ]
    out_shape = [
        jax.ShapeDtypeStruct((bsz, seq, D_HEADS_PAD), BF16),
        jax.ShapeDtypeStruct((bsz, seq, D_HEADS_PAD), BF16),
        jax.ShapeDtypeStruct((bsz, seq, D_HEADS_PAD), BF16),
        jax.ShapeDtypeStruct((bsz, seq, D_LRU), BF16),
        jax.ShapeDtypeStruct((bsz, seq, KV_RANK), F32),
        jax.ShapeDtypeStruct((bsz, seq, D_ROPE), F32),
        jax.ShapeDtypeStruct((ns, bsz, 1, D_LRU), F32),
        jax.ShapeDtypeStruct((ns, bsz, CONV_W - 1, D_LRU), F32),
    ]
    if apply_ln0:
        out_specs = [tile(D_MODEL)] + out_specs
        out_shape = [jax.ShapeDtypeStruct((bsz, seq, D_MODEL), F32)] + out_shape
    return pl.pallas_call(
        functools.partial(_prompt_pre_kernel, apply_ln0, tm),
        grid=(ns, bsz),
        in_specs=in_specs,
        out_specs=out_specs,
        out_shape=out_shape,
        scratch_shapes=[
            pltpu.VMEM((bsz, 1, D_LRU), F32),
            pltpu.VMEM((bsz, SUBLANES, D_LRU), F32),
            pltpu.VMEM((tm + SUBLANES, D_LRU), F32),
            pltpu.VMEM((tm, D_LRU), F32),
            pltpu.VMEM((tm, D_LRU), F32),
            pltpu.VMEM((tm, D_LRU), F32),
        ],
        compiler_params=pltpu.CompilerParams(
            dimension_semantics=("arbitrary", "arbitrary"), vmem_limit_bytes=VMEM_LIMIT),
        name="prompt_pre_ln0" if apply_ln0 else "prompt_pre",
    )(x, mod, ln0_g, ln0_b, lw["w_in"], lw["conv_w"], lw["conv_b"], lw["w_gates"], lw["b_a"],
      lw["b_x"], lw["lam"], lw["q_norm_g"], lw["w_uq"], lw["kv_norm_g"], lw["w_kv"],
      lw["g_lru"], cq_tab, sq_tab, ck_tab, sk_tab)


def _attn_kernel(tq, q_ref, k_ref, v_ref, o_ref):
    qi = pl.program_id(2)
    q = q_ref[0]

    def tile_step(j, carry, masked):
        m_prev, l_prev, acc = carry
        off = pl.multiple_of(j * tq, tq)
        kt = k_ref[0, pl.ds(off, tq), :]
        vt = v_ref[0, pl.ds(off, tq), :]
        s = lax.dot_general(q, kt, (((1,), (1,)), ((), ())), preferred_element_type=F32)
        if masked:
            row = lax.broadcasted_iota(jnp.int32, (tq, tq), 0)
            col = lax.broadcasted_iota(jnp.int32, (tq, tq), 1)
            s = jnp.where(col <= row, s, -jnp.inf)
        m_new = jnp.maximum(m_prev, jnp.max(s, axis=1, keepdims=True))
        alpha = jnp.exp2((m_prev - m_new) * EXP2_SCALE)
        p = jnp.exp2((s - m_new) * EXP2_SCALE)
        l_new = alpha * l_prev + jnp.sum(p, axis=1, keepdims=True)
        acc = alpha * acc + _bdot(p.astype(BF16), vt)
        return m_new, l_new, acc

    init = (jnp.full((tq, 1), -jnp.inf, F32), jnp.zeros((tq, 1), F32),
            jnp.zeros((tq, HEAD_PAD), F32))
    carry = lax.fori_loop(0, qi, lambda j, c: tile_step(j, c, False), init)
    _, l_fin, acc = tile_step(qi, carry, True)
    o_ref[0] = acc / l_fin


def _prompt_attention(q, k, v):
    bsz, seq, _ = q.shape
    tq = _prompt_tile(seq)
    return pl.pallas_call(
        functools.partial(_attn_kernel, tq),
        grid=(bsz, N_HEADS, seq // tq),
        in_specs=[
            pl.BlockSpec((1, tq, HEAD_PAD), lambda b, h, i: (b, i, h)),
            pl.BlockSpec((1, seq, HEAD_PAD), lambda b, h, i: (b, 0, h)),
            pl.BlockSpec((1, seq, HEAD_PAD), lambda b, h, i: (b, 0, h)),
        ],
        out_specs=pl.BlockSpec((1, tq, HEAD_PAD), lambda b, h, i: (b, i, h)),
        out_shape=jax.ShapeDtypeStruct((bsz, seq, D_HEADS_PAD), F32),
        compiler_params=pltpu.CompilerParams(
            dimension_semantics=("arbitrary", "arbitrary", "arbitrary"),
            vmem_limit_bytes=VMEM_LIMIT),
        name="prompt_attention",
    )(q, k, v)


def _post_kernel(from_latent, ff_chunk, x_ref, mod_ref, mixl_ref, att_ref, wuv_ref, gatt_ref,
                 wol_ref, woa_ref, lng_ref, lnb_ref, wup_ref, wdn_ref, o_ref):
    x = x_ref[0]
    if from_latent:
        ya = jnp.concatenate(
            [_bdot(att_ref[hh], wuv_ref[hh]) for hh in range(N_HEADS)], axis=1)
    else:
        ya = att_ref[0]
    na = _rms_norm(ya, gatt_ref[...], N_HEADS * D_V).astype(BF16)
    o = _bdot(mixl_ref[0], wol_ref[...]) + _bdot(na, woa_ref[...])
    x1 = _layer_norm(ALPHA * x + mod_ref[2, 0] * o, lng_ref[0:1, :], lnb_ref[0:1, :])
    h2 = (x1 * (1.0 + mod_ref[4, 0]) + mod_ref[3, 0]).astype(BF16)
    f = jnp.zeros(x.shape, F32)
    for c in range(D_FF // ff_chunk):
        up = _bdot(h2, wup_ref[:, c * ff_chunk:(c + 1) * ff_chunk])
        act = jnp.square(jnp.maximum(up, 0.0)).astype(BF16)
        f = f + _bdot(act, wdn_ref[c * ff_chunk:(c + 1) * ff_chunk, :])
    o_ref[0] = _layer_norm(ALPHA * x1 + mod_ref[5, 0] * f, lng_ref[1:2, :], lnb_ref[1:2, :])


def _post(from_latent, x, mod, mixl, att, lw):
    groups, rows, _ = x.shape
    tm = _prompt_tile(rows)
    tile = lambda w: pl.BlockSpec((1, tm, w), lambda g, s: (g, s, 0))
    mod_rows = mod.shape[2]
    if from_latent:
        att_spec = pl.BlockSpec((N_HEADS, tm, KV_RANK), lambda g, s: (0, s, 0))
    else:
        att_spec = tile(D_HEADS_PAD)
    return pl.pallas_call(
        functools.partial(_post_kernel, from_latent, 1024),
        grid=(groups, rows // tm),
        in_specs=[
            tile(D_MODEL),
            pl.BlockSpec((6, 1, mod_rows, D_MODEL), lambda g, s: (0, g, 0, 0)),
            tile(D_LRU), att_spec,
            _const_spec(lw["w_uv_heads"].shape), _const_spec((1, D_HEADS_PAD)),
            _const_spec(lw["w_o_lru"].shape), _const_spec(lw["w_o_att"].shape),
            _const_spec((2, D_MODEL)), _const_spec((2, D_MODEL)),
            _const_spec(lw["w_up"].shape), _const_spec(lw["w_down"].shape),
        ],
        out_specs=tile(D_MODEL),
        out_shape=jax.ShapeDtypeStruct(x.shape, F32),
        compiler_params=pltpu.CompilerParams(
            dimension_semantics=("arbitrary", "arbitrary"), vmem_limit_bytes=VMEM_LIMIT),
        name="sample_post" if from_latent else "prompt_post",
    )(x, mod, mixl, att, lw["w_uv_heads"], lw["g_att"], lw["w_o_lru"], lw["w_o_att"],
      lw["ln_g"], lw["ln_b"], lw["w_up"], lw["w_down"])


def _sample_pre_kernel(apply_ln0, *refs):
    (x_ref, mod_ref, ln0g_ref, ln0b_ref, win_ref, convw_ref, convb_ref, wg_ref, ba_ref,
     bx_ref, lam_ref, qg_ref, wuq_ref, kvg_ref, wukt_ref, glru_ref,
     cq_ref, sq_ref, ck_ref, sk_ref, h0_ref, buf_ref) = refs[:22]
    outs = refs[22:]
    if apply_ln0:
        xn_ref, outs = outs[0], outs[1:]
    q_ref, qlat_ref, mixl_ref, ckv_ref, kr_ref, hnew_ref, conv_ref = outs

    x = x_ref[...]
    if apply_ln0:
        x = _layer_norm(x, ln0g_ref[...], ln0b_ref[...])
        xn_ref[...] = x
    h = (x * (1.0 + mod_ref[1, 0]) + mod_ref[0, 0]).astype(BF16)
    proj = _bdot(h, win_ref[...])
    u_raw = proj[:, OFF_U:OFF_GATE]
    gate = proj[:, OFF_GATE:OFF_CQ]

    uc = convb_ref[...] + u_raw * convw_ref[CONV_W - 1:CONV_W, :]
    for kk in range(CONV_W - 1):
        uc = uc + buf_ref[kk] * convw_ref[kk:kk + 1, :]
    for kk in range(CONV_W - 2):
        conv_ref[kk] = buf_ref[kk + 1]
    conv_ref[CONV_W - 2] = u_raw

    a, bb = _lru_coeffs(uc, wg_ref, ba_ref, bx_ref, lam_ref)
    h_new = a * h0_ref[...] + bb
    hnew_ref[...] = h_new
    y = h_new * _gelu_tanh(gate)
    mixl_ref[...] = _rms_norm(y, glru_ref[...], D_LRU).astype(BF16)

    q, ckvn, krw = _latent_heads(proj, cq_ref[0:1, :], sq_ref[0:1, :], ck_ref[0:1, :],
                                 sk_ref[0:1, :], qg_ref, wuq_ref, kvg_ref)
    qb = q.astype(BF16)
    q_ref[...] = qb
    ckv_ref[...] = ckvn
    kr_ref[...] = krw[:, :D_ROPE]
    for hh in range(N_HEADS):
        qlat_ref[hh] = _bdot(qb[:, hh * HEAD_PAD:(hh + 1) * HEAD_PAD], wukt_ref[hh])


def _sample_pre(apply_ln0, x, mod, ln0_g, ln0_b, lw, tabs, h0, buf):
    rows = x.shape[0]
    cq_tab, sq_tab, ck_tab, sk_tab = tabs
    full = lambda a: _const_spec(a.shape)
    args = (x, mod, ln0_g, ln0_b, lw["w_in"], lw["conv_w"], lw["conv_b"], lw["w_gates"],
            lw["b_a"], lw["b_x"], lw["lam"], lw["q_norm_g"], lw["w_uq"], lw["kv_norm_g"],
            lw["w_uk_t"], lw["g_lru"], cq_tab, sq_tab, ck_tab, sk_tab, h0, buf)
    out_shape = [
        jax.ShapeDtypeStruct((rows, D_HEADS_PAD), BF16),
        jax.ShapeDtypeStruct((N_HEADS, rows, KV_RANK), F32),
        jax.ShapeDtypeStruct((rows, D_LRU), BF16),
        jax.ShapeDtypeStruct((rows, KV_RANK), F32),
        jax.ShapeDtypeStruct((rows, D_ROPE), F32),
        jax.ShapeDtypeStruct((rows, D_LRU), F32),
        jax.ShapeDtypeStruct((CONV_W - 1, rows, D_LRU), F32),
    ]
    if apply_ln0:
        out_shape = [jax.ShapeDtypeStruct((rows, D_MODEL), F32)] + out_shape
    return pl.pallas_call(
        functools.partial(_sample_pre_kernel, apply_ln0),
        grid=(1,),
        in_specs=[full(a) for a in args],
        out_specs=[_const_spec(s.shape) for s in out_shape],
        out_shape=out_shape,
        compiler_params=pltpu.CompilerParams(
            dimension_semantics=("arbitrary",), vmem_limit_bytes=VMEM_LIMIT),
        name="sample_pre_ln0" if apply_ln0 else "sample_pre",
    )(*args)


def _paged_attn_kernel(layer, pages_per_step, steps_per_seq, page_size,
                       pt_ref, qlat_ref, qrope_ref, cnew_ref, knew_ref, cache_c, cache_k,
                       o_ref, cbuf, kbuf, sem_c, sem_k, m_sc, l_sc, acc_sc):
    step = pl.program_id(0)
    n_steps = pl.num_programs(0)
    slot = step % 2
    chunk = step % steps_per_seq

    def page_copies(st, sl, j):
        seq = st // steps_per_seq
        page = pt_ref[seq, (st % steps_per_seq) * pages_per_step + j]
        rows = pl.ds(j * page_size, page_size)
        return (pltpu.make_async_copy(cache_c.at[layer, page], cbuf.at[sl, rows], sem_c.at[sl]),
                pltpu.make_async_copy(cache_k.at[layer, page], kbuf.at[sl, rows], sem_k.at[sl]))

    def start_step(st, sl):
        for j in range(pages_per_step):
            for cp in page_copies(st, sl, j):
                cp.start()

    @pl.when(step == 0)
    def _():
        start_step(step, slot)

    @pl.when(step + 1 < n_steps)
    def _():
        start_step(step + 1, 1 - slot)

    qlat = qlat_ref[0]
    qrope = qrope_ref[0]

    @pl.when(chunk == 0)
    def _():
        cnew = cnew_ref[0].astype(BF16).astype(F32)
        knew = knew_ref[0].astype(BF16).astype(F32)
        s_new = (jnp.sum(qlat.astype(F32) * cnew, axis=1, keepdims=True)
                 + jnp.sum(qrope.astype(F32) * knew, axis=1, keepdims=True))
        m_sc[...] = s_new
        l_sc[...] = jnp.ones_like(s_new)
        acc_sc[...] = jnp.broadcast_to(cnew, acc_sc.shape)

    for j in range(pages_per_step):
        for cp in page_copies(step, slot, j):
            cp.wait()

    ck = cbuf[slot].astype(BF16)
    kk = kbuf[slot].astype(BF16)
    dn = (((1,), (1,)), ((), ()))
    s = (lax.dot_general(qlat, ck, dn, preferred_element_type=F32)
         + lax.dot_general(qrope, kk, dn, preferred_element_type=F32))
    m_prev = m_sc[...]
    m_new = jnp.maximum(m_prev, jnp.max(s, axis=1, keepdims=True))
    alpha = jnp.exp2((m_prev - m_new) * EXP2_SCALE)
    p = jnp.exp2((s - m_new) * EXP2_SCALE)
    l_new = alpha * l_sc[...] + jnp.sum(p, axis=1, keepdims=True)
    acc = alpha * acc_sc[...] + _bdot(p.astype(BF16), ck)
    m_sc[...] = m_new
    l_sc[...] = l_new
    acc_sc[...] = acc

    @pl.when(chunk == steps_per_seq - 1)
    def _():
        o_ref[0] = acc / l_new


def _paged_pages_per_step(n_pages):
    for p in (16, 8, 4, 2, 1):
        if n_pages % p == 0:
            return p
    return 1


def _paged_attention(layer, page_table, qlat, qrope, cnew, knew, cache_ckv, cache_krope):
    nseq, n_pages = page_table.shape
    page_size = cache_ckv.shape[2]
    pps = _paged_pages_per_step(n_pages)
    steps_per_seq = n_pages // pps
    per_seq = lambda w: pl.BlockSpec((1, N_HEADS, w), lambda s, pt: (s // steps_per_seq, 0, 0))
    one_row = lambda w: pl.BlockSpec((1, 1, w), lambda s, pt: (s // steps_per_seq, 0, 0))
    grid_spec = pltpu.PrefetchScalarGridSpec(
        num_scalar_prefetch=1,
        grid=(nseq * steps_per_seq,),
        in_specs=[
            per_seq(KV_RANK), per_seq(D_ROPE), one_row(KV_RANK), one_row(D_ROPE),
            pl.BlockSpec(memory_space=pl.ANY), pl.BlockSpec(memory_space=pl.ANY),
        ],
        out_specs=per_seq(KV_RANK),
        scratch_shapes=[
            pltpu.VMEM((2, pps * page_size, KV_RANK), F32),
            pltpu.VMEM((2, pps * page_size, D_ROPE), F32),
            pltpu.SemaphoreType.DMA((2,)),
            pltpu.SemaphoreType.DMA((2,)),
            pltpu.VMEM((N_HEADS, 1), F32),
            pltpu.VMEM((N_HEADS, 1), F32),
            pltpu.VMEM((N_HEADS, KV_RANK), F32),
        ],
    )
    return pl.pallas_call(
        functools.partial(_paged_attn_kernel, layer, pps, steps_per_seq, page_size),
        grid_spec=grid_spec,
        out_shape=jax.ShapeDtypeStruct((nseq, N_HEADS, KV_RANK), F32),
        compiler_params=pltpu.CompilerParams(
            dimension_semantics=("arbitrary",), vmem_limit_bytes=VMEM_LIMIT),
        name="paged_attention",
    )(page_table, qlat, qrope, cnew, knew, cache_ckv, cache_krope)


def _pad_heads(w, width):
    pad = [(0, 0)] * (w.ndim - 1) + [(0, HEAD_PAD - width)]
    return jnp.pad(w, pad).reshape(*w.shape[:-2], D_HEADS_PAD)


def _block_diag(w):
    eye = jnp.eye(LRU_BLOCKS, dtype=w.dtype)
    return jnp.einsum("nij,nm->nimj", w, eye).reshape(D_LRU, D_LRU)


def _layer_weights(l, p):
    w_in = p["w_in"][l]
    kr_cols = w_in[:, OFF_KRA:OFF_KRA + D_ROPE]
    kr_swap = jnp.concatenate([-kr_cols[:, HALF_ROPE:], kr_cols[:, :HALF_ROPE]], axis=1)
    lane_pad = lambda w: jnp.pad(w, ((0, 0), (0, LANES - w.shape[1])))
    w_in_pad = jnp.concatenate([w_in[:, :OFF_KRA], lane_pad(kr_cols), lane_pad(kr_swap)], axis=1)

    w_uq = p["w_uq"][l]
    x1 = w_uq[..., D_NOPE:D_NOPE + HALF_ROPE]
    x2 = w_uq[..., D_NOPE + HALF_ROPE:]
    w_uq_swap = jnp.concatenate([jnp.zeros_like(w_uq[..., :D_NOPE]), -x2, x1], axis=-1)
    w_uq_pad = jnp.concatenate([_pad_heads(w_uq, D_QK), _pad_heads(w_uq_swap, D_QK)], axis=1)

    w_uk = p["w_uk"][l]
    w_uv = p["w_uv"][l]
    place = np.zeros((LANES, N_HEADS, HEAD_PAD), np.float32)
    for j in range(D_ROPE):
        place[j, :, D_NOPE + j] = 1.0
    w_k = jnp.concatenate([_pad_heads(w_uk, D_NOPE), jnp.asarray(place.reshape(LANES, -1))], axis=0)
    w_v = jnp.concatenate([_pad_heads(w_uv, D_V), jnp.zeros((LANES, D_HEADS_PAD), F32)], axis=0)
    w_kv = jnp.concatenate([w_k, w_v], axis=1)

    w_uk_t = jnp.pad(jnp.transpose(w_uk, (1, 2, 0)), ((0, 0), (0, HEAD_PAD - D_NOPE), (0, 0)))
    w_uv_heads = jnp.pad(jnp.transpose(w_uv, (1, 0, 2)), ((0, 0), (0, 0), (0, HEAD_PAD - D_V)))

    w_o = p["w_o"][l]
    w_o_att = jnp.pad(w_o[D_LRU:].reshape(N_HEADS, D_V, D_MODEL),
                      ((0, 0), (0, HEAD_PAD - D_V), (0, 0))).reshape(D_HEADS_PAD, D_MODEL)
    g_att = _pad_heads(p["g_att"][l].reshape(1, N_HEADS, D_V), D_V)
    row = lambda v: v.reshape(1, -1)
    return {
        "w_in": w_in_pad.astype(BF16),
        "conv_w": p["conv_w"][l], "conv_b": row(p["conv_b"][l]),
        "w_gates": jnp.concatenate([_block_diag(p["w_a"][l]), _block_diag(p["w_x"][l])],
                                   axis=1).astype(BF16),
        "b_a": row(p["b_a"][l]), "b_x": row(p["b_x"][l]), "lam": row(p["lru_lambda"][l]),
        "q_norm_g": row(p["q_norm_g"][l]), "w_uq": w_uq_pad.astype(BF16),
        "kv_norm_g": row(p["kv_norm_g"][l]), "w_kv": w_kv.astype(BF16),
        "w_uk_t": w_uk_t.astype(BF16), "w_uv_heads": w_uv_heads.astype(BF16),
        "g_lru": row(p["g_lru"][l]), "g_att": g_att,
        "w_o_lru": w_o[:D_LRU].astype(BF16), "w_o_att": w_o_att.astype(BF16),
        "ln_g": p["ln_g"][l], "ln_b": p["ln_b"][l],
        "w_up": p["w_up"][l].astype(BF16), "w_down": p["w_down"][l].astype(BF16),
    }


def kernel(x_prompt, x_sample, cache_ckv, cache_krope, state_lru_h, state_conv, page_table,
           c_prompt, c_sample, ln0_g, ln0_b, w_ada, b_ada, w_in, conv_w, conv_b, w_a, b_a,
           w_x, b_x, lru_lambda, q_norm_g, w_uq, kv_norm_g, w_uk, w_uv, g_lru, g_att, w_o,
           ln_g, ln_b, w_up, w_down):
    params = dict(w_in=w_in, conv_w=conv_w, conv_b=conv_b, w_a=w_a, b_a=b_a, w_x=w_x, b_x=b_x,
                  lru_lambda=lru_lambda, q_norm_g=q_norm_g, w_uq=w_uq, kv_norm_g=kv_norm_g,
                  w_uk=w_uk, w_uv=w_uv, g_lru=g_lru, g_att=g_att, w_o=w_o, ln_g=ln_g,
                  ln_b=ln_b, w_up=w_up, w_down=w_down)
    b_p, s_p, _ = x_prompt.shape
    b_s, s_s, _ = x_sample.shape
    if s_s != 1:
        raise ValueError("the sample group is a single-token decode step")
    depth = w_ada.shape[0]
    past_len = page_table.shape[1] * cache_ckv.shape[2]

    mods = _ada_mod(jnp.concatenate([c_prompt, c_sample], axis=0), w_ada, b_ada)
    mods = mods.reshape(depth, b_p + b_s, 6, D_MODEL)
    q_rows, k_rows = _rope_lane_rows()
    tabs_p = _rope_tables(s_p, 0, *q_rows) + _rope_tables(s_p, 0, *k_rows)
    tabs_s = _rope_tables(SUBLANES, past_len, *q_rows) + _rope_tables(SUBLANES, past_len, *k_rows)
    ln0g, ln0b = ln0_g.reshape(1, -1), ln0_b.reshape(1, -1)

    xp = x_prompt
    xs = x_sample.reshape(b_s, D_MODEL)
    outs_p = [[] for _ in range(4)]
    outs_s = [[] for _ in range(4)]
    for l in range(depth):
        lw = _layer_weights(l, params)
        first = l == 0
        mod_p = jnp.transpose(mods[l, :b_p], (1, 0, 2)).reshape(6, b_p, 1, D_MODEL)
        mod_s = jnp.transpose(mods[l, b_p:], (1, 0, 2)).reshape(6, 1, b_s, D_MODEL)

        res = _prompt_pre(first, xp, mod_p, ln0g, ln0b, lw, tabs_p)
        if first:
            xp, res = res[0], res[1:]
        q, k, v, mixl, ckv, kr, h_last, conv = res
        att = _prompt_attention(q, k, v)
        xp = _post(False, xp, mod_p, mixl, att, lw)
        for dst, val in zip(outs_p, (ckv, kr, h_last[-1].reshape(b_p, D_LRU), conv[-1])):
            dst.append(val)

        res = _sample_pre(first, xs, mod_s, ln0g, ln0b, lw, tabs_s, state_lru_h[l],
                          jnp.transpose(state_conv[l], (1, 0, 2)))
        if first:
            xs, res = res[0], res[1:]
        q_s, qlat, mixl_s, ckv_s, kr_s, h_new, conv_s = res
        qlat_b = jnp.transpose(qlat, (1, 0, 2)).astype(BF16)
        qrope_b = q_s.reshape(b_s, N_HEADS, HEAD_PAD)[:, :, D_NOPE:D_QK]
        o_lat = _paged_attention(l, page_table, qlat_b, qrope_b, ckv_s.reshape(b_s, 1, KV_RANK),
                                 kr_s.reshape(b_s, 1, D_ROPE), cache_ckv, cache_krope)
        o_lat_h = jnp.transpose(o_lat, (1, 0, 2)).astype(BF16)
        xs = _post(True, xs.reshape(1, b_s, D_MODEL), mod_s, mixl_s.reshape(1, b_s, D_LRU),
                   o_lat_h, lw).reshape(b_s, D_MODEL)
        for dst, val in zip(outs_s, (ckv_s.reshape(b_s, 1, KV_RANK), kr_s.reshape(b_s, 1, D_ROPE),
                                     h_new, jnp.transpose(conv_s, (1, 0, 2)))):
            dst.append(val)

    stack = lambda vals: jnp.stack(vals, axis=0)
    return (xp, xs.reshape(b_s, 1, D_MODEL),
            stack(outs_p[0]), stack(outs_p[1]), stack(outs_p[2]), stack(outs_p[3]),
            stack(outs_s[0]), stack(outs_s[1]), stack(outs_s[2]), stack(outs_s[3]))
```

```python
import functools
import math

import jax
import jax.numpy as jnp
import numpy as np
from jax import lax
from jax.experimental import pallas as pl
from jax.experimental.pallas import tpu as pltpu

D_MODEL = 1024
D_LRU = 512
LRU_BLOCKS = 8
LRU_BLOCK = D_LRU // LRU_BLOCKS
CONV_W = 4
LRU_C = 8.0
N_HEADS = 8
D_NOPE = 64
D_ROPE = 32
D_QK = D_NOPE + D_ROPE
D_V = 64
Q_RANK = 384
KV_RANK = 256
ROPE_THETA = 10000.0
D_FF = 4 * D_MODEL
DEPTH = 2
ALPHA = (2 * DEPTH) ** 0.25
ATT_SCALE = D_QK ** -0.5
EPS = 1e-6

LANES = 128
SUBLANES = 8
VMEM_BYTES_V7X = 64 * 1024 * 1024
VMEM_LIMIT = VMEM_BYTES_V7X - 8 * 1024 * 1024

HEAD_PAD = LANES
D_HEADS_PAD = N_HEADS * HEAD_PAD
HALF_ROPE = D_ROPE // 2
OFF_U, OFF_GATE, OFF_CQ, OFF_CKV = 0, D_LRU, 2 * D_LRU, 2 * D_LRU + Q_RANK
OFF_KRA = OFF_CKV + KV_RANK
OFF_KRB = OFF_KRA + LANES
D_IN_PAD = OFF_KRB + LANES
EXP2_SCALE = ATT_SCALE * math.log2(math.e)

F32 = jnp.float32
BF16 = jnp.bfloat16


def _prompt_tile(seq):
    for t in (512, 256, 128, 64, 32, 16, 8):
        if seq % t == 0:
            return t
    raise ValueError(f"sequence length {seq} must be a multiple of 8")


def _const_spec(shape):
    nd = len(shape)
    return pl.BlockSpec(shape, lambda *_: (0,) * nd, pipeline_mode=pl.Buffered(1))


def _layer_norm(x, g, b):
    mu = jnp.mean(x, axis=-1, keepdims=True)
    xc = x - mu
    var = jnp.mean(xc * xc, axis=-1, keepdims=True)
    return xc * lax.rsqrt(var + EPS) * g + b


def _rms_norm(x, g, n):
    ms = jnp.sum(x * x, axis=-1, keepdims=True) * (1.0 / n)
    return x * lax.rsqrt(ms + EPS) * g


def _sigmoid(x):
    return 1.0 / (1.0 + jnp.exp(-x))


def _softplus(x):
    return jnp.maximum(x, 0.0) + jnp.log1p(jnp.exp(-jnp.abs(x)))


def _gelu_tanh(x):
    c = math.sqrt(2.0 / math.pi)
    return 0.5 * x * (1.0 + jnp.tanh(c * (x + 0.044715 * (x * x * x))))


def _bdot(a, b):
    return jnp.dot(a, b, preferred_element_type=F32)


def _lru_coeffs(uc, wg_ref, ba_ref, bx_ref, lam_ref):
    g = _bdot(uc.astype(BF16), wg_ref[...])
    r = _sigmoid(g[:, :D_LRU] + ba_ref[...])
    i = _sigmoid(g[:, D_LRU:] + bx_ref[...])
    log_a = (-LRU_C) * r * _softplus(-lam_ref[...])
    a = jnp.exp(log_a)
    t = jnp.tanh(log_a)
    b = jnp.sqrt((-2.0 * t) / (1.0 - t)) * (i * uc)
    return a, b


def _latent_heads(proj, cq_tab, sq_tab, ck_tab, sk_tab, qg_ref, wuq_ref, kvg_ref):
    cqn = _rms_norm(proj[:, OFF_CQ:OFF_CKV], qg_ref[...], Q_RANK)
    qq = _bdot(cqn.astype(BF16), wuq_ref[...])
    q = qq[:, :D_HEADS_PAD] * cq_tab + qq[:, D_HEADS_PAD:] * sq_tab
    ckvn = _rms_norm(proj[:, OFF_CKV:OFF_KRA], kvg_ref[...], KV_RANK)
    krw = proj[:, OFF_KRA:OFF_KRB] * ck_tab + proj[:, OFF_KRB:D_IN_PAD] * sk_tab
    return q, ckvn, krw


def _ada_kernel(c_ref, w_ref, b_ref, o_ref):
    c = c_ref[...]
    s = (c * _sigmoid(c)).astype(BF16)
    o_ref[0] = _bdot(s, w_ref[0].astype(BF16)) + b_ref[0]


def _ada_mod(c_all, w_ada, b_ada):
    rows = c_all.shape[0]
    depth, _, n_out = w_ada.shape
    tn = 1536
    return pl.pallas_call(
        _ada_kernel,
        grid=(depth, n_out // tn),
        in_specs=[
            pl.BlockSpec((rows, D_MODEL), lambda l, j: (0, 0)),
            pl.BlockSpec((1, D_MODEL, tn), lambda l, j: (l, 0, j)),
            pl.BlockSpec((1, 1, tn), lambda l, j: (l, 0, j)),
        ],
        out_specs=pl.BlockSpec((1, rows, tn), lambda l, j: (l, 0, j)),
        out_shape=jax.ShapeDtypeStruct((depth, rows, n_out), F32),
        compiler_params=pltpu.CompilerParams(
            dimension_semantics=("arbitrary", "arbitrary"), vmem_limit_bytes=VMEM_LIMIT),
        name="ada_mod",
    )(c_all, w_ada, b_ada.reshape(depth, 1, n_out))


def _rope_table_kernel(pos0, inv_ref, one_ref, rope_ref, cos_ref, sin_ref):
    rows = cos_ref.shape[0]
    pos = (lax.broadcasted_iota(jnp.int32, (rows, 1), 0)
           + (pl.program_id(0) * rows + pos0)).astype(F32)
    ang = pos * inv_ref[...]
    cos_ref[...] = rope_ref[...] * jnp.cos(ang) + one_ref[...]
    sin_ref[...] = rope_ref[...] * jnp.sin(ang)


def _rope_tables(rows, pos0, inv_row, one_row, rope_row):
    width = inv_row.shape[1]
    tr = _prompt_tile(rows)
    return pl.pallas_call(
        functools.partial(_rope_table_kernel, pos0),
        grid=(rows // tr,),
        in_specs=[_const_spec((1, width))] * 3,
        out_specs=[pl.BlockSpec((tr, width), lambda i: (i, 0))] * 2,
        out_shape=[jax.ShapeDtypeStruct((rows, width), F32)] * 2,
        compiler_params=pltpu.CompilerParams(dimension_semantics=("arbitrary",)),
        name="rope_tables",
    )(inv_row, one_row, rope_row)


def _rope_lane_rows():
    inv = ROPE_THETA ** (-jnp.arange(HALF_ROPE, dtype=F32) / HALF_ROPE)
    zeros = lambda n: jnp.zeros((n,), F32)
    ones = lambda n: jnp.ones((n,), F32)
    head_inv = jnp.concatenate([zeros(D_NOPE), inv, inv, zeros(HEAD_PAD - D_QK)])
    head_rope = jnp.concatenate([zeros(D_NOPE), ones(D_ROPE), zeros(HEAD_PAD - D_QK)])
    head_one = jnp.concatenate([ones(D_NOPE), zeros(HEAD_PAD - D_NOPE)])
    key_inv = jnp.concatenate([inv, inv, zeros(LANES - D_ROPE)])
    key_rope = jnp.concatenate([ones(D_ROPE), zeros(LANES - D_ROPE)])
    row = lambda v, reps: jnp.tile(v, reps)[None, :]
    q_rows = (row(head_inv, N_HEADS), row(head_one, N_HEADS), row(head_rope, N_HEADS))
    k_rows = (row(key_inv, 1), row(zeros(LANES), 1), row(key_rope, 1))
    return q_rows, k_rows


def _prompt_pre_kernel(apply_ln0, tm, *refs):
    (x_ref, mod_ref, ln0g_ref, ln0b_ref, win_ref, convw_ref, convb_ref, wg_ref, ba_ref,
     bx_ref, lam_ref, qg_ref, wuq_ref, kvg_ref, wkv_ref, glru_ref,
     cq_ref, sq_ref, ck_ref, sk_ref) = refs[:20]
    outs = refs[20:]
    if apply_ln0:
        xn_ref, outs = outs[0], outs[1:]
    (q_ref, k_ref, v_ref, mixl_ref, ckv_ref, kr_ref, hlast_ref, conv_ref,
     hcar_ref, halo_ref, ubuf_ref, a_sc, b_sc, hs_sc) = outs

    s_idx = pl.program_id(0)
    b_idx = pl.program_id(1)

    x = x_ref[0]
    if apply_ln0:
        x = _layer_norm(x, ln0g_ref[...], ln0b_ref[...])
        xn_ref[0] = x
    h = (x * (1.0 + mod_ref[1, 0]) + mod_ref[0, 0]).astype(BF16)
    proj = _bdot(h, win_ref[...])
    u_raw = proj[:, OFF_U:OFF_GATE]
    gate = proj[:, OFF_GATE:OFF_CQ]

    @pl.when(s_idx == 0)
    def _():
        halo_ref[b_idx] = jnp.zeros((SUBLANES, D_LRU), F32)
        hcar_ref[b_idx] = jnp.zeros((1, D_LRU), F32)

    ubuf_ref[0:SUBLANES, :] = halo_ref[b_idx]
    ubuf_ref[SUBLANES:, :] = u_raw
    halo_ref[b_idx] = u_raw[tm - SUBLANES:, :]
    uc = convb_ref[...] + u_raw * convw_ref[CONV_W - 1:CONV_W, :]
    for kk in range(CONV_W - 1):
        off = SUBLANES - (CONV_W - 1) + kk
        uc = uc + ubuf_ref[off:off + tm, :] * convw_ref[kk:kk + 1, :]
    conv_ref[0, 0] = ubuf_ref[tm + SUBLANES - (CONV_W - 1):tm + SUBLANES, :]

    a, bb = _lru_coeffs(uc, wg_ref, ba_ref, bx_ref, lam_ref)

    row8 = lax.broadcasted_iota(jnp.int32, (tm, D_LRU), 0) & (SUBLANES - 1)
    for d in (1, 2, 4):
        a_sh = pltpu.roll(a, d, axis=0)
        b_sh = pltpu.roll(bb, d, axis=0)
        take = row8 >= d
        bb = jnp.where(take, a * b_sh + bb, bb)
        a = jnp.where(take, a * a_sh, a)
    a_sc[...] = a
    b_sc[...] = bb

    def group_step(g, hprev):
        off = pl.multiple_of(g * SUBLANES, SUBLANES)
        hg = a_sc[pl.ds(off, SUBLANES), :] * hprev + b_sc[pl.ds(off, SUBLANES), :]
        hs_sc[pl.ds(off, SUBLANES), :] = hg
        return hg[SUBLANES - 1:SUBLANES, :]

    h_fin = lax.fori_loop(0, tm // SUBLANES, group_step, hcar_ref[b_idx], unroll=8)
    hcar_ref[b_idx] = h_fin
    hlast_ref[0, 0] = h_fin

    y = hs_sc[...] * _gelu_tanh(gate)
    mixl_ref[0] = _rms_norm(y, glru_ref[...], D_LRU).astype(BF16)

    q, ckvn, krw = _latent_heads(proj, cq_ref[...], sq_ref[...], ck_ref[...], sk_ref[...],
                                 qg_ref, wuq_ref, kvg_ref)
    q_ref[0] = q.astype(BF16)
    ckv_ref[0] = ckvn
    kr_ref[0] = krw[:, :D_ROPE]
    kv = _bdot(jnp.concatenate([ckvn, krw], axis=1).astype(BF16), wkv_ref[...])
    k_ref[0] = kv[:, :D_HEADS_PAD].astype(BF16)
    lane = lax.broadcasted_iota(jnp.int32, (1, D_HEADS_PAD), 1) & (HEAD_PAD - 1)
    v_ref[0] = (kv[:, D_HEADS_PAD:] + (lane == D_V).astype(F32)).astype(BF16)


def _prompt_pre(apply_ln0, x, mod, ln0_g, ln0_b, lw, tabs):
    bsz, seq, _ = x.shape
    tm = _prompt_tile(seq)
    ns = seq // tm
    cq_tab, sq_tab, ck_tab, sk_tab = tabs
    tile = lambda w: pl.BlockSpec((1, tm, w), lambda s, b: (b, s, 0))
    in_specs = [
        tile(D_MODEL),
        pl.BlockSpec((6, 1, 1, D_MODEL), lambda s, b: (0, b, 0, 0)),
        _const_spec((1, D_MODEL)), _const_spec((1, D_MODEL)),
        _const_spec(lw["w_in"].shape), _const_spec((CONV_W, D_LRU)), _const_spec((1, D_LRU)),
        _const_spec(lw["w_gates"].shape), _const_spec((1, D_LRU)), _const_spec((1, D_LRU)),
        _const_spec((1, D_LRU)), _const_spec((1, Q_RANK)), _const_spec(lw["w_uq"].shape),
        _const_spec((1, KV_RANK)), _const_spec(lw["w_kv"].shape), _const_spec((1, D_LRU)),
        pl.BlockSpec((tm, D_HEADS_PAD), lambda s, b: (s, 0)),
        pl.BlockSpec((tm, D_HEADS_PAD), lambda s, b: (s, 0)),
        pl.BlockSpec((tm, LANES), lambda s, b: (s, 0)),
        pl.BlockSpec((tm, LANES), lambda s, b: (s, 0)),
    ]
    out_specs = [
        tile(D_HEADS_PAD), tile(D_HEADS_PAD), tile(D_HEADS_PAD), tile(D_LRU), tile(KV_RANK),
        tile(D_ROPE),
        pl.BlockSpec((1, 1, 1, D_LRU), lambda s, b: (s, b, 0, 0)),
        pl.BlockSpec((1, 1, CONV_W - 1, D_LRU), lambda s, b: (s, b, 0, 0)),
    ]
    out_shape = [
        jax.ShapeDtypeStruct((bsz, seq, D_HEADS_PAD), BF16),
        jax.ShapeDtypeStruct((bsz, seq, D_HEADS_PAD), BF16),
        jax.ShapeDtypeStruct((bsz, seq, D_HEADS_PAD), BF16),
        jax.ShapeDtypeStruct((bsz, seq, D_LRU), BF16),
        jax.ShapeDtypeStruct((bsz, seq, KV_RANK), F32),
        jax.ShapeDtypeStruct((bsz, seq, D_ROPE), F32),
        jax.ShapeDtypeStruct((ns, bsz, 1, D_LRU), F32),
        jax.ShapeDtypeStruct((ns, bsz, CONV_W - 1, D_LRU), F32),
    ]
    if apply_ln0:
        out_specs = [tile(D_MODEL)] + out_specs
        out_shape = [jax.ShapeDtypeStruct((bsz, seq, D_MODEL), F32)] + out_shape
    return pl.pallas_call(
        functools.partial(_prompt_pre_kernel, apply_ln0, tm),
        grid=(ns, bsz),
        in_specs=in_specs,
        out_specs=out_specs,
        out_shape=out_shape,
        scratch_shapes=[
            pltpu.VMEM((bsz, 1, D_LRU), F32),
            pltpu.VMEM((bsz, SUBLANES, D_LRU), F32),
            pltpu.VMEM((tm + SUBLANES, D_LRU), F32),
            pltpu.VMEM((tm, D_LRU), F32),
            pltpu.VMEM((tm, D_LRU), F32),
            pltpu.VMEM((tm, D_LRU), F32),
        ],
        compiler_params=pltpu.CompilerParams(
            dimension_semantics=("arbitrary", "arbitrary"), vmem_limit_bytes=VMEM_LIMIT),
        name="prompt_pre_ln0" if apply_ln0 else "prompt_pre",
    )(x, mod, ln0_g, ln0_b, lw["w_in"], lw["conv_w"], lw["conv_b"], lw["w_gates"], lw["b_a"],
      lw["b_x"], lw["lam"], lw["q_norm_g"], lw["w_uq"], lw["kv_norm_g"], lw["w_kv"],
      lw["g_lru"], cq_tab, sq_tab, ck_tab, sk_tab)


ATTN_HEADS_PER_STEP = 4


def _attn_kernel(tq, q_ref, k_ref, v_ref, o_ref):
    qi = pl.program_id(2)
    heads = range(ATTN_HEADS_PER_STEP)
    lanes = lambda hh: slice(hh * HEAD_PAD, (hh + 1) * HEAD_PAD)
    qs = [q_ref[0, :, lanes(hh)] for hh in heads]

    def tile_step(j, carry, masked):
        off = pl.multiple_of(j * tq, tq)
        out = []
        for hh in heads:
            m_prev, acc = carry[hh]
            kt = k_ref[0, pl.ds(off, tq), lanes(hh)]
            vt = v_ref[0, pl.ds(off, tq), lanes(hh)]
            s = lax.dot_general(qs[hh], kt, (((1,), (1,)), ((), ())),
                                preferred_element_type=F32)
            if masked:
                row = lax.broadcasted_iota(jnp.int32, (tq, tq), 0)
                col = lax.broadcasted_iota(jnp.int32, (tq, tq), 1)
                s = jnp.where(col <= row, s, -jnp.inf)
            m_new = jnp.maximum(m_prev, jnp.max(s, axis=1, keepdims=True))
            alpha = jnp.exp2((m_prev - m_new) * EXP2_SCALE)
            p = jnp.exp2((s - m_new) * EXP2_SCALE).astype(BF16)
            out.append((m_new, alpha * acc + _bdot(p, vt)))
        return tuple(out)

    init = tuple((jnp.full((tq, 1), -jnp.inf, F32), jnp.zeros((tq, HEAD_PAD), F32))
                 for _ in heads)
    carry = lax.fori_loop(0, qi, lambda j, c: tile_step(j, c, False), init)
    final = tile_step(qi, carry, True)
    value_lane = lax.broadcasted_iota(jnp.int32, (tq, HEAD_PAD), 1) < D_V
    for hh in heads:
        acc = final[hh][1]
        o_ref[0, :, lanes(hh)] = jnp.where(value_lane, acc / acc[:, D_V:D_V + 1], 0.0)


def _prompt_attention(q, k, v):
    bsz, seq, _ = q.shape
    tq = _prompt_tile(seq)
    width = ATTN_HEADS_PER_STEP * HEAD_PAD
    return pl.pallas_call(
        functools.partial(_attn_kernel, tq),
        grid=(bsz, N_HEADS // ATTN_HEADS_PER_STEP, seq // tq),
        in_specs=[
            pl.BlockSpec((1, tq, width), lambda b, h, i: (b, i, h)),
            pl.BlockSpec((1, seq, width), lambda b, h, i: (b, 0, h)),
            pl.BlockSpec((1, seq, width), lambda b, h, i: (b, 0, h)),
        ],
        out_specs=pl.BlockSpec((1, tq, width), lambda b, h, i: (b, i, h)),
        out_shape=jax.ShapeDtypeStruct((bsz, seq, D_HEADS_PAD), F32),
        compiler_params=pltpu.CompilerParams(
            dimension_semantics=("arbitrary", "arbitrary", "arbitrary"),
            vmem_limit_bytes=VMEM_LIMIT),
        name="prompt_attention",
    )(q, k, v)


def _post_kernel(from_latent, ff_chunk, x_ref, mod_ref, mixl_ref, att_ref, wuv_ref, gatt_ref,
                 wol_ref, woa_ref, lng_ref, lnb_ref, wup_ref, wdn_ref, o_ref):
    x = x_ref[0]
    if from_latent:
        ya = jnp.concatenate(
            [_bdot(att_ref[hh], wuv_ref[hh]) for hh in range(N_HEADS)], axis=1)
    else:
        ya = att_ref[0]
    na = _rms_norm(ya, gatt_ref[...], N_HEADS * D_V).astype(BF16)
    o = _bdot(mixl_ref[0], wol_ref[...]) + _bdot(na, woa_ref[...])
    x1 = _layer_norm(ALPHA * x + mod_ref[2, 0] * o, lng_ref[0:1, :], lnb_ref[0:1, :])
    h2 = (x1 * (1.0 + mod_ref[4, 0]) + mod_ref[3, 0]).astype(BF16)
    f = jnp.zeros(x.shape, F32)
    for c in range(D_FF // ff_chunk):
        up = _bdot(h2, wup_ref[:, c * ff_chunk:(c + 1) * ff_chunk])
        act = jnp.square(jnp.maximum(up, 0.0)).astype(BF16)
        f = f + _bdot(act, wdn_ref[c * ff_chunk:(c + 1) * ff_chunk, :])
    o_ref[0] = _layer_norm(ALPHA * x1 + mod_ref[5, 0] * f, lng_ref[1:2, :], lnb_ref[1:2, :])


def _post(from_latent, x, mod, mixl, att, lw):
    groups, rows, _ = x.shape
    tm = _prompt_tile(rows)
    tile = lambda w: pl.BlockSpec((1, tm, w), lambda g, s: (g, s, 0))
    mod_rows = mod.shape[2]
    if from_latent:
        att_spec = pl.BlockSpec((N_HEADS, tm, KV_RANK), lambda g, s: (0, s, 0))
    else:
        att_spec = tile(D_HEADS_PAD)
    return pl.pallas_call(
        functools.partial(_post_kernel, from_latent, 1024),
        grid=(groups, rows // tm),
        in_specs=[
            tile(D_MODEL),
            pl.BlockSpec((6, 1, mod_rows, D_MODEL), lambda g, s: (0, g, 0, 0)),
            tile(D_LRU), att_spec,
            _const_spec(lw["w_uv_heads"].shape), _const_spec((1, D_HEADS_PAD)),
            _const_spec(lw["w_o_lru"].shape), _const_spec(lw["w_o_att"].shape),
            _const_spec((2, D_MODEL)), _const_spec((2, D_MODEL)),
            _const_spec(lw["w_up"].shape), _const_spec(lw["w_down"].shape),
        ],
        out_specs=tile(D_MODEL),
        out_shape=jax.ShapeDtypeStruct(x.shape, F32),
        compiler_params=pltpu.CompilerParams(
            dimension_semantics=("arbitrary", "arbitrary"), vmem_limit_bytes=VMEM_LIMIT),
        name="sample_post" if from_latent else "prompt_post",
    )(x, mod, mixl, att, lw["w_uv_heads"], lw["g_att"], lw["w_o_lru"], lw["w_o_att"],
      lw["ln_g"], lw["ln_b"], lw["w_up"], lw["w_down"])


def _sample_pre_kernel(apply_ln0, *refs):
    (x_ref, mod_ref, ln0g_ref, ln0b_ref, win_ref, convw_ref, convb_ref, wg_ref, ba_ref,
     bx_ref, lam_ref, qg_ref, wuq_ref, kvg_ref, wukt_ref, glru_ref,
     cq_ref, sq_ref, ck_ref, sk_ref, h0_ref, buf_ref) = refs[:22]
    outs = refs[22:]
    if apply_ln0:
        xn_ref, outs = outs[0], outs[1:]
    q_ref, qlat_ref, mixl_ref, ckv_ref, kr_ref, hnew_ref, conv_ref = outs

    x = x_ref[...]
    if apply_ln0:
        x = _layer_norm(x, ln0g_ref[...], ln0b_ref[...])
        xn_ref[...] = x
    h = (x * (1.0 + mod_ref[1, 0]) + mod_ref[0, 0]).astype(BF16)
    proj = _bdot(h, win_ref[...])
    u_raw = proj[:, OFF_U:OFF_GATE]
    gate = proj[:, OFF_GATE:OFF_CQ]

    uc = convb_ref[...] + u_raw * convw_ref[CONV_W - 1:CONV_W, :]
    for kk in range(CONV_W - 1):
        uc = uc + buf_ref[kk] * convw_ref[kk:kk + 1, :]
    for kk in range(CONV_W - 2):
        conv_ref[kk] = buf_ref[kk + 1]
    conv_ref[CONV_W - 2] = u_raw

    a, bb = _lru_coeffs(uc, wg_ref, ba_ref, bx_ref, lam_ref)
    h_new = a * h0_ref[...] + bb
    hnew_ref[...] = h_new
    y = h_new * _gelu_tanh(gate)
    mixl_ref[...] = _rms_norm(y, glru_ref[...], D_LRU).astype(BF16)

    q, ckvn, krw = _latent_heads(proj, cq_ref[0:1, :], sq_ref[0:1, :], ck_ref[0:1, :],
                                 sk_ref[0:1, :], qg_ref, wuq_ref, kvg_ref)
    qb = q.astype(BF16)
    q_ref[...] = qb
    ckv_ref[...] = ckvn
    kr_ref[...] = krw[:, :D_ROPE]
    for hh in range(N_HEADS):
        qlat_ref[hh] = _bdot(qb[:, hh * HEAD_PAD:(hh + 1) * HEAD_PAD], wukt_ref[hh])


def _sample_pre(apply_ln0, x, mod, ln0_g, ln0_b, lw, tabs, h0, buf):
    rows = x.shape[0]
    cq_tab, sq_tab, ck_tab, sk_tab = tabs
    full = lambda a: _const_spec(a.shape)
    args = (x, mod, ln0_g, ln0_b, lw["w_in"], lw["conv_w"], lw["conv_b"], lw["w_gates"],
            lw["b_a"], lw["b_x"], lw["lam"], lw["q_norm_g"], lw["w_uq"], lw["kv_norm_g"],
            lw["w_uk_t"], lw["g_lru"], cq_tab, sq_tab, ck_tab, sk_tab, h0, buf)
    out_shape = [
        jax.ShapeDtypeStruct((rows, D_HEADS_PAD), BF16),
        jax.ShapeDtypeStruct((N_HEADS, rows, KV_RANK), F32),
        jax.ShapeDtypeStruct((rows, D_LRU), BF16),
        jax.ShapeDtypeStruct((rows, KV_RANK), F32),
        jax.ShapeDtypeStruct((rows, D_ROPE), F32),
        jax.ShapeDtypeStruct((rows, D_LRU), F32),
        jax.ShapeDtypeStruct((CONV_W - 1, rows, D_LRU), F32),
    ]
    if apply_ln0:
        out_shape = [jax.ShapeDtypeStruct((rows, D_MODEL), F32)] + out_shape
    return pl.pallas_call(
        functools.partial(_sample_pre_kernel, apply_ln0),
        grid=(1,),
        in_specs=[full(a) for a in args],
        out_specs=[_const_spec(s.shape) for s in out_shape],
        out_shape=out_shape,
        compiler_params=pltpu.CompilerParams(
            dimension_semantics=("arbitrary",), vmem_limit_bytes=VMEM_LIMIT),
        name="sample_pre_ln0" if apply_ln0 else "sample_pre",
    )(*args)


def _paged_attn_kernel(layer, pages_per_chunk, chunks_per_seq, page_size,
                       pt_ref, qlat_ref, qrope_ref, cnew_ref, knew_ref, cache_c, cache_kt,
                       o_ref, cbuf0, cbuf1, kbuf0, kbuf1, sem_c, sem_k, m_sc, l_sc, acc_sc):
    g = pl.program_id(0)
    n_steps = pl.num_programs(0)
    total_chunks = 2 * n_steps
    bufs = ((cbuf0, kbuf0), (cbuf1, kbuf1))

    def page_copies(chunk_idx, sl, j):
        seq = chunk_idx // chunks_per_seq
        page = pt_ref[seq, (chunk_idx % chunks_per_seq) * pages_per_chunk + j]
        span = pl.ds(j * page_size, page_size)
        cb, kb = bufs[sl]
        return (pltpu.make_async_copy(cache_c.at[layer, page], cb.at[span, :], sem_c.at[sl]),
                pltpu.make_async_copy(cache_kt.at[layer, page], kb.at[:, span], sem_k.at[sl]))

    def start_chunk(chunk_idx, sl):
        for j in range(pages_per_chunk):
            for cp in page_copies(chunk_idx, sl, j):
                cp.start()

    def wait_chunk(chunk_idx, sl):
        for j in range(pages_per_chunk):
            for cp in page_copies(chunk_idx, sl, j):
                cp.wait()

    qlat = qlat_ref[0]
    qrope = qrope_ref[0]

    def consume(sl, m_prev, l_prev, acc_prev):
        cb, kb = bufs[sl]
        ck = cb[...].astype(BF16)
        s = (lax.dot_general(qlat, ck, (((1,), (1,)), ((), ())), preferred_element_type=F32)
             + _bdot(qrope, kb[...].astype(BF16)))
        m_new = jnp.maximum(m_prev, jnp.max(s, axis=1, keepdims=True))
        alpha = jnp.exp2((m_prev - m_new) * EXP2_SCALE)
        p = jnp.exp2((s - m_new) * EXP2_SCALE)
        l_new = alpha * l_prev + jnp.sum(p, axis=1, keepdims=True)
        acc = alpha * acc_prev + _bdot(p.astype(BF16), ck)
        return m_new, l_new, acc

    c_even = 2 * g
    c_odd = c_even + 1
    c_next = lax.rem(c_odd + 1, total_chunks)

    @pl.when(g == 0)
    def _():
        start_chunk(c_even, 0)

    wait_chunk(c_even, 0)
    start_chunk(c_odd, 1)

    cnew = cnew_ref[0].astype(BF16).astype(F32)
    knew = knew_ref[0].astype(BF16).astype(F32)
    s_new = (jnp.sum(qlat.astype(F32) * cnew, axis=1, keepdims=True)
             + jnp.sum(qrope.astype(F32) * knew, axis=1, keepdims=True))
    opens = lax.rem(c_even, chunks_per_seq) == 0
    m0 = jnp.where(opens, s_new, m_sc[...])
    l0 = jnp.where(opens, jnp.ones_like(s_new), l_sc[...])
    acc0 = jnp.where(opens, jnp.broadcast_to(cnew, acc_sc.shape), acc_sc[...])
    m1, l1, acc1 = consume(0, m0, l0, acc0)

    wait_chunk(c_odd, 1)
    start_chunk(c_next, 0)
    m2, l2, acc2 = consume(1, m1, l1, acc1)

    m_sc[...] = m2
    l_sc[...] = l2
    acc_sc[...] = acc2
    o_ref[0] = acc2 / l2

    @pl.when(g == n_steps - 1)
    def _():
        wait_chunk(c_next, 0)


def _paged_pages_per_chunk(n_pages):
    for p in (16, 8, 4, 2, 1):
        if n_pages % (2 * p) == 0:
            return p
    raise ValueError(f"the page count per sequence ({n_pages}) must be even")


def _paged_attention(layer, page_table, qlat, qrope, cnew, knew, cache_ckv, cache_krope_t):
    nseq, n_pages = page_table.shape
    page_size = cache_ckv.shape[2]
    ppc = _paged_pages_per_chunk(n_pages)
    chunks_per_seq = n_pages // ppc
    steps_per_seq = chunks_per_seq // 2
    per_seq = lambda w: pl.BlockSpec((1, N_HEADS, w), lambda s, pt: (s // steps_per_seq, 0, 0))
    one_row = lambda w: pl.BlockSpec((1, 1, w), lambda s, pt: (s // steps_per_seq, 0, 0))
    grid_spec = pltpu.PrefetchScalarGridSpec(
        num_scalar_prefetch=1,
        grid=(nseq * steps_per_seq,),
        in_specs=[
            per_seq(KV_RANK), per_seq(D_ROPE), one_row(KV_RANK), one_row(D_ROPE),
            pl.BlockSpec(memory_space=pl.ANY), pl.BlockSpec(memory_space=pl.ANY),
        ],
        out_specs=per_seq(KV_RANK),
        scratch_shapes=[
            pltpu.VMEM((ppc * page_size, KV_RANK), F32),
            pltpu.VMEM((ppc * page_size, KV_RANK), F32),
            pltpu.VMEM((D_ROPE, ppc * page_size), F32),
            pltpu.VMEM((D_ROPE, ppc * page_size), F32),
            pltpu.SemaphoreType.DMA((2,)),
            pltpu.SemaphoreType.DMA((2,)),
            pltpu.VMEM((N_HEADS, 1), F32),
            pltpu.VMEM((N_HEADS, 1), F32),
            pltpu.VMEM((N_HEADS, KV_RANK), F32),
        ],
    )
    return pl.pallas_call(
        functools.partial(_paged_attn_kernel, layer, ppc, chunks_per_seq, page_size),
        grid_spec=grid_spec,
        out_shape=jax.ShapeDtypeStruct((nseq, N_HEADS, KV_RANK), F32),
        compiler_params=pltpu.CompilerParams(
            dimension_semantics=("arbitrary",), vmem_limit_bytes=VMEM_LIMIT),
        name="paged_attention",
    )(page_table, qlat, qrope, cnew, knew, cache_ckv, cache_krope_t)


def _pad_heads(w, width):
    pad = [(0, 0)] * (w.ndim - 1) + [(0, HEAD_PAD - width)]
    return jnp.pad(w, pad).reshape(*w.shape[:-2], D_HEADS_PAD)


def _block_diag(w):
    eye = jnp.eye(LRU_BLOCKS, dtype=w.dtype)
    return jnp.einsum("nij,nm->nimj", w, eye).reshape(D_LRU, D_LRU)


def _layer_weights(l, p):
    w_in = p["w_in"][l]
    kr_cols = w_in[:, OFF_KRA:OFF_KRA + D_ROPE]
    kr_swap = jnp.concatenate([-kr_cols[:, HALF_ROPE:], kr_cols[:, :HALF_ROPE]], axis=1)
    lane_pad = lambda w: jnp.pad(w, ((0, 0), (0, LANES - w.shape[1])))
    w_in_pad = jnp.concatenate([w_in[:, :OFF_KRA], lane_pad(kr_cols), lane_pad(kr_swap)], axis=1)

    w_uq = p["w_uq"][l]
    x1 = w_uq[..., D_NOPE:D_NOPE + HALF_ROPE]
    x2 = w_uq[..., D_NOPE + HALF_ROPE:]
    w_uq_swap = jnp.concatenate([jnp.zeros_like(w_uq[..., :D_NOPE]), -x2, x1], axis=-1)
    w_uq_pad = jnp.concatenate([_pad_heads(w_uq, D_QK), _pad_heads(w_uq_swap, D_QK)], axis=1)

    w_uk = p["w_uk"][l]
    w_uv = p["w_uv"][l]
    place = np.zeros((LANES, N_HEADS, HEAD_PAD), np.float32)
    for j in range(D_ROPE):
        place[j, :, D_NOPE + j] = 1.0
    w_k = jnp.concatenate([_pad_heads(w_uk, D_NOPE), jnp.asarray(place.reshape(LANES, -1))], axis=0)
    w_v = jnp.concatenate([_pad_heads(w_uv, D_V), jnp.zeros((LANES, D_HEADS_PAD), F32)], axis=0)
    w_kv = jnp.concatenate([w_k, w_v], axis=1)

    w_uk_t = jnp.pad(jnp.transpose(w_uk, (1, 2, 0)), ((0, 0), (0, HEAD_PAD - D_NOPE), (0, 0)))
    w_uv_heads = jnp.pad(jnp.transpose(w_uv, (1, 0, 2)), ((0, 0), (0, 0), (0, HEAD_PAD - D_V)))

    w_o = p["w_o"][l]
    w_o_att = jnp.pad(w_o[D_LRU:].reshape(N_HEADS, D_V, D_MODEL),
                      ((0, 0), (0, HEAD_PAD - D_V), (0, 0))).reshape(D_HEADS_PAD, D_MODEL)
    g_att = _pad_heads(p["g_att"][l].reshape(1, N_HEADS, D_V), D_V)
    row = lambda v: v.reshape(1, -1)
    return {
        "w_in": w_in_pad.astype(BF16),
        "conv_w": p["conv_w"][l], "conv_b": row(p["conv_b"][l]),
        "w_gates": jnp.concatenate([_block_diag(p["w_a"][l]), _block_diag(p["w_x"][l])],
                                   axis=1).astype(BF16),
        "b_a": row(p["b_a"][l]), "b_x": row(p["b_x"][l]), "lam": row(p["lru_lambda"][l]),
        "q_norm_g": row(p["q_norm_g"][l]), "w_uq": w_uq_pad.astype(BF16),
        "kv_norm_g": row(p["kv_norm_g"][l]), "w_kv": w_kv.astype(BF16),
        "w_uk_t": w_uk_t.astype(BF16), "w_uv_heads": w_uv_heads.astype(BF16),
        "g_lru": row(p["g_lru"][l]), "g_att": g_att,
        "w_o_lru": w_o[:D_LRU].astype(BF16), "w_o_att": w_o_att.astype(BF16),
        "ln_g": p["ln_g"][l], "ln_b": p["ln_b"][l],
        "w_up": p["w_up"][l].astype(BF16), "w_down": p["w_down"][l].astype(BF16),
    }


def kernel(x_prompt, x_sample, cache_ckv, cache_krope, state_lru_h, state_conv, page_table,
           c_prompt, c_sample, ln0_g, ln0_b, w_ada, b_ada, w_in, conv_w, conv_b, w_a, b_a,
           w_x, b_x, lru_lambda, q_norm_g, w_uq, kv_norm_g, w_uk, w_uv, g_lru, g_att, w_o,
           ln_g, ln_b, w_up, w_down):
    params = dict(w_in=w_in, conv_w=conv_w, conv_b=conv_b, w_a=w_a, b_a=b_a, w_x=w_x, b_x=b_x,
                  lru_lambda=lru_lambda, q_norm_g=q_norm_g, w_uq=w_uq, kv_norm_g=kv_norm_g,
                  w_uk=w_uk, w_uv=w_uv, g_lru=g_lru, g_att=g_att, w_o=w_o, ln_g=ln_g,
                  ln_b=ln_b, w_up=w_up, w_down=w_down)
    b_p, s_p, _ = x_prompt.shape
    b_s, s_s, _ = x_sample.shape
    if s_s != 1:
        raise ValueError("the sample group is a single-token decode step")
    depth = w_ada.shape[0]
    past_len = page_table.shape[1] * cache_ckv.shape[2]

    mods = _ada_mod(jnp.concatenate([c_prompt, c_sample], axis=0), w_ada, b_ada)
    mods = mods.reshape(depth, b_p + b_s, 6, D_MODEL)
    q_rows, k_rows = _rope_lane_rows()
    tabs_p = _rope_tables(s_p, 0, *q_rows) + _rope_tables(s_p, 0, *k_rows)
    tabs_s = _rope_tables(SUBLANES, past_len, *q_rows) + _rope_tables(SUBLANES, past_len, *k_rows)
    ln0g, ln0b = ln0_g.reshape(1, -1), ln0_b.reshape(1, -1)
    cache_krope_t = jnp.swapaxes(cache_krope, 2, 3)

    xp = x_prompt
    xs = x_sample.reshape(b_s, D_MODEL)
    outs_p = [[] for _ in range(4)]
    outs_s = [[] for _ in range(4)]
    for l in range(depth):
        lw = _layer_weights(l, params)
        first = l == 0
        mod_p = jnp.transpose(mods[l, :b_p], (1, 0, 2)).reshape(6, b_p, 1, D_MODEL)
        mod_s = jnp.transpose(mods[l, b_p:], (1, 0, 2)).reshape(6, 1, b_s, D_MODEL)

        res = _prompt_pre(first, xp, mod_p, ln0g, ln0b, lw, tabs_p)
        if first:
            xp, res = res[0], res[1:]
        q, k, v, mixl, ckv, kr, h_last, conv = res
        att = _prompt_attention(q, k, v)
        xp = _post(False, xp, mod_p, mixl, att, lw)
        for dst, val in zip(outs_p, (ckv, kr, h_last[-1].reshape(b_p, D_LRU), conv[-1])):
            dst.append(val)

        res = _sample_pre(first, xs, mod_s, ln0g, ln0b, lw, tabs_s, state_lru_h[l],
                          jnp.transpose(state_conv[l], (1, 0, 2)))
        if first:
            xs, res = res[0], res[1:]
        q_s, qlat, mixl_s, ckv_s, kr_s, h_new, conv_s = res
        qlat_b = jnp.transpose(qlat, (1, 0, 2)).astype(BF16)
        qrope_b = q_s.reshape(b_s, N_HEADS, HEAD_PAD)[:, :, D_NOPE:D_QK]
        o_lat = _paged_attention(l, page_table, qlat_b, qrope_b, ckv_s.reshape(b_s, 1, KV_RANK),
                                 kr_s.reshape(b_s, 1, D_ROPE), cache_ckv, cache_krope_t)
        o_lat_h = jnp.transpose(o_lat, (1, 0, 2)).astype(BF16)
        xs = _post(True, xs.reshape(1, b_s, D_MODEL), mod_s, mixl_s.reshape(1, b_s, D_LRU),
                   o_lat_h, lw).reshape(b_s, D_MODEL)
        for dst, val in zip(outs_s, (ckv_s.reshape(b_s, 1, KV_RANK), kr_s.reshape(b_s, 1, D_ROPE),
                                     h_new, jnp.transpose(conv_s, (1, 0, 2)))):
            dst.append(val)

    stack = lambda vals: jnp.stack(vals, axis=0)
    return (xp, xs.reshape(b_s, 1, D_MODEL),
            stack(outs_p[0]), stack(outs_p[1]), stack(outs_p[2]), stack(outs_p[3]),
            stack(outs_s[0]), stack(outs_s[1]), stack(outs_s[2]), stack(outs_s[3]))
```

```python
import functools
import math

import jax
import jax.numpy as jnp
import numpy as np
from jax import lax
from jax.experimental import pallas as pl
from jax.experimental.pallas import tpu as pltpu

D_MODEL = 1024
D_LRU = 512
LRU_BLOCKS = 8
LRU_BLOCK = D_LRU // LRU_BLOCKS
CONV_W = 4
LRU_C = 8.0
N_HEADS = 8
D_NOPE = 64
D_ROPE = 32
D_QK = D_NOPE + D_ROPE
D_V = 64
Q_RANK = 384
KV_RANK = 256
ROPE_THETA = 10000.0
D_FF = 4 * D_MODEL
DEPTH = 2
ALPHA = (2 * DEPTH) ** 0.25
ATT_SCALE = D_QK ** -0.5
EPS = 1e-6

LANES = 128
SUBLANES = 8
VMEM_BYTES_V7X = 64 * 1024 * 1024
VMEM_LIMIT = VMEM_BYTES_V7X - 8 * 1024 * 1024

HEAD_PAD = LANES
D_HEADS_PAD = N_HEADS * HEAD_PAD
HALF_ROPE = D_ROPE // 2
OFF_U, OFF_GATE, OFF_CQ, OFF_CKV = 0, D_LRU, 2 * D_LRU, 2 * D_LRU + Q_RANK
OFF_KRA = OFF_CKV + KV_RANK
OFF_KRB = OFF_KRA + LANES
D_IN_PAD = OFF_KRB + LANES
EXP2_SCALE = ATT_SCALE * math.log2(math.e)

F32 = jnp.float32
BF16 = jnp.bfloat16


def _prompt_tile(seq):
    for t in (512, 256, 128, 64, 32, 16, 8):
        if seq % t == 0:
            return t
    raise ValueError(f"sequence length {seq} must be a multiple of 8")


def _const_spec(shape):
    nd = len(shape)
    return pl.BlockSpec(shape, lambda *_: (0,) * nd, pipeline_mode=pl.Buffered(1))


def _layer_norm(x, g, b):
    mu = jnp.mean(x, axis=-1, keepdims=True)
    xc = x - mu
    var = jnp.mean(xc * xc, axis=-1, keepdims=True)
    return xc * lax.rsqrt(var + EPS) * g + b


def _rms_norm(x, g, n):
    ms = jnp.sum(x * x, axis=-1, keepdims=True) * (1.0 / n)
    return x * lax.rsqrt(ms + EPS) * g


def _sigmoid(x):
    return 1.0 / (1.0 + jnp.exp(-x))


def _softplus(x):
    return jnp.maximum(x, 0.0) + jnp.log1p(jnp.exp(-jnp.abs(x)))


def _gelu_tanh(x):
    c = math.sqrt(2.0 / math.pi)
    return 0.5 * x * (1.0 + jnp.tanh(c * (x + 0.044715 * (x * x * x))))


def _bdot(a, b):
    return jnp.dot(a, b, preferred_element_type=F32)


def _lru_coeffs(uc, wg_ref, ba_ref, bx_ref, lam_ref):
    g = _bdot(uc.astype(BF16), wg_ref[...])
    r = _sigmoid(g[:, :D_LRU] + ba_ref[...])
    i = _sigmoid(g[:, D_LRU:] + bx_ref[...])
    log_a = (-LRU_C) * r * _softplus(-lam_ref[...])
    a = jnp.exp(log_a)
    t = jnp.tanh(log_a)
    b = jnp.sqrt((-2.0 * t) / (1.0 - t)) * (i * uc)
    return a, b


def _latent_heads(proj, cq_tab, sq_tab, ck_tab, sk_tab, qg_ref, wuq_ref, kvg_ref):
    cqn = _rms_norm(proj[:, OFF_CQ:OFF_CKV], qg_ref[...], Q_RANK)
    qq = _bdot(cqn.astype(BF16), wuq_ref[...])
    q = qq[:, :D_HEADS_PAD] * cq_tab + qq[:, D_HEADS_PAD:] * sq_tab
    ckvn = _rms_norm(proj[:, OFF_CKV:OFF_KRA], kvg_ref[...], KV_RANK)
    krw = proj[:, OFF_KRA:OFF_KRB] * ck_tab + proj[:, OFF_KRB:D_IN_PAD] * sk_tab
    return q, ckvn, krw


def _ada_kernel(c_ref, w_ref, b_ref, o_ref):
    c = c_ref[...]
    s = (c * _sigmoid(c)).astype(BF16)
    o_ref[0] = _bdot(s, w_ref[0].astype(BF16)) + b_ref[0]


def _ada_mod(c_all, w_ada, b_ada):
    rows = c_all.shape[0]
    depth, _, n_out = w_ada.shape
    tn = 1536
    return pl.pallas_call(
        _ada_kernel,
        grid=(depth, n_out // tn),
        in_specs=[
            pl.BlockSpec((rows, D_MODEL), lambda l, j: (0, 0)),
            pl.BlockSpec((1, D_MODEL, tn), lambda l, j: (l, 0, j)),
            pl.BlockSpec((1, 1, tn), lambda l, j: (l, 0, j)),
        ],
        out_specs=pl.BlockSpec((1, rows, tn), lambda l, j: (l, 0, j)),
        out_shape=jax.ShapeDtypeStruct((depth, rows, n_out), F32),
        compiler_params=pltpu.CompilerParams(
            dimension_semantics=("arbitrary", "arbitrary"), vmem_limit_bytes=VMEM_LIMIT),
        name="ada_mod",
    )(c_all, w_ada, b_ada.reshape(depth, 1, n_out))


def _rope_table_kernel(pos0, reps, inv_ref, one_ref, rope_ref, cos_ref, sin_ref):
    rows = cos_ref.shape[0]
    pos = (lax.broadcasted_iota(jnp.int32, (rows, 1), 0)
           + (pl.program_id(0) * rows + pos0)).astype(F32)
    ang = pos * inv_ref[...]
    cos = rope_ref[...] * jnp.cos(ang) + one_ref[...]
    sin = rope_ref[...] * jnp.sin(ang)
    cos_ref[...] = jnp.concatenate([cos] * reps, axis=1)
    sin_ref[...] = jnp.concatenate([sin] * reps, axis=1)


def _rope_tables(rows, pos0, reps, inv_row, one_row, rope_row):
    tr = _prompt_tile(rows)
    return pl.pallas_call(
        functools.partial(_rope_table_kernel, pos0, reps),
        grid=(rows // tr,),
        in_specs=[_const_spec((1, LANES))] * 3,
        out_specs=[pl.BlockSpec((tr, reps * LANES), lambda i: (i, 0))] * 2,
        out_shape=[jax.ShapeDtypeStruct((rows, reps * LANES), F32)] * 2,
        compiler_params=pltpu.CompilerParams(dimension_semantics=("arbitrary",)),
        name="rope_tables",
    )(inv_row, one_row, rope_row)


def _rope_lane_rows():
    inv = ROPE_THETA ** (-jnp.arange(HALF_ROPE, dtype=F32) / HALF_ROPE)
    zeros = lambda n: jnp.zeros((n,), F32)
    ones = lambda n: jnp.ones((n,), F32)
    head_inv = jnp.concatenate([zeros(D_NOPE), inv, inv, zeros(HEAD_PAD - D_QK)])
    head_rope = jnp.concatenate([zeros(D_NOPE), ones(D_ROPE), zeros(HEAD_PAD - D_QK)])
    head_one = jnp.concatenate([ones(D_NOPE), zeros(HEAD_PAD - D_NOPE)])
    key_inv = jnp.concatenate([inv, inv, zeros(LANES - D_ROPE)])
    key_rope = jnp.concatenate([ones(D_ROPE), zeros(LANES - D_ROPE)])
    row = lambda v: v[None, :]
    q_rows = (row(head_inv), row(head_one), row(head_rope))
    k_rows = (row(key_inv), row(zeros(LANES)), row(key_rope))
    return q_rows, k_rows


def _prompt_pre_kernel(apply_ln0, tm, *refs):
    (x_ref, mod_ref, ln0g_ref, ln0b_ref, win_ref, convw_ref, convb_ref, wg_ref, ba_ref,
     bx_ref, lam_ref, qg_ref, wuq_ref, kvg_ref, wkv_ref, glru_ref,
     cq_ref, sq_ref, ck_ref, sk_ref) = refs[:20]
    outs = refs[20:]
    if apply_ln0:
        xn_ref, outs = outs[0], outs[1:]
    (q_ref, k_ref, v_ref, mixl_ref, ckv_ref, kr_ref, hlast_ref, conv_ref,
     hcar_ref, halo_ref, ubuf_ref, a_sc, b_sc, hs_sc) = outs

    s_idx = pl.program_id(0)
    b_idx = pl.program_id(1)

    x = x_ref[0]
    if apply_ln0:
        x = _layer_norm(x, ln0g_ref[...], ln0b_ref[...])
        xn_ref[0] = x
    h = (x * (1.0 + mod_ref[1, 0]) + mod_ref[0, 0]).astype(BF16)
    proj = _bdot(h, win_ref[...])
    u_raw = proj[:, OFF_U:OFF_GATE]
    gate = proj[:, OFF_GATE:OFF_CQ]

    @pl.when(s_idx == 0)
    def _():
        halo_ref[b_idx] = jnp.zeros((SUBLANES, D_LRU), F32)
        hcar_ref[b_idx] = jnp.zeros((1, D_LRU), F32)

    ubuf_ref[0:SUBLANES, :] = halo_ref[b_idx]
    ubuf_ref[SUBLANES:, :] = u_raw
    halo_ref[b_idx] = u_raw[tm - SUBLANES:, :]
    uc = convb_ref[...] + u_raw * convw_ref[CONV_W - 1:CONV_W, :]
    for kk in range(CONV_W - 1):
        off = SUBLANES - (CONV_W - 1) + kk
        uc = uc + ubuf_ref[off:off + tm, :] * convw_ref[kk:kk + 1, :]
    conv_ref[0, 0] = ubuf_ref[tm + SUBLANES - (CONV_W - 1):tm + SUBLANES, :]

    a, bb = _lru_coeffs(uc, wg_ref, ba_ref, bx_ref, lam_ref)

    row8 = lax.broadcasted_iota(jnp.int32, (tm, D_LRU), 0) & (SUBLANES - 1)
    for d in (1, 2, 4):
        a_sh = pltpu.roll(a, d, axis=0)
        b_sh = pltpu.roll(bb, d, axis=0)
        take = row8 >= d
        bb = jnp.where(take, a * b_sh + bb, bb)
        a = jnp.where(take, a * a_sh, a)
    a_sc[...] = a
    b_sc[...] = bb

    def group_step(g, hprev):
        off = pl.multiple_of(g * SUBLANES, SUBLANES)
        hg = a_sc[pl.ds(off, SUBLANES), :] * hprev + b_sc[pl.ds(off, SUBLANES), :]
        hs_sc[pl.ds(off, SUBLANES), :] = hg
        return hg[SUBLANES - 1:SUBLANES, :]

    h_fin = lax.fori_loop(0, tm // SUBLANES, group_step, hcar_ref[b_idx], unroll=8)
    hcar_ref[b_idx] = h_fin
    hlast_ref[0, 0] = h_fin

    y = hs_sc[...] * _gelu_tanh(gate)
    mixl_ref[0] = _rms_norm(y, glru_ref[...], D_LRU).astype(BF16)

    q, ckvn, krw = _latent_heads(proj, cq_ref[...], sq_ref[...], ck_ref[...], sk_ref[...],
                                 qg_ref, wuq_ref, kvg_ref)
    q_ref[0] = q.astype(BF16)
    ckv_ref[0] = ckvn
    kr_ref[0] = krw[:, :D_ROPE]
    kv = _bdot(jnp.concatenate([ckvn, krw], axis=1).astype(BF16), wkv_ref[...])
    k_ref[0] = kv[:, :D_HEADS_PAD].astype(BF16)
    lane = lax.broadcasted_iota(jnp.int32, (1, D_HEADS_PAD), 1) & (HEAD_PAD - 1)
    v_ref[0] = (kv[:, D_HEADS_PAD:] + (lane == D_V).astype(F32)).astype(BF16)


def _prompt_pre(apply_ln0, x, mod, ln0_g, ln0_b, lw, tabs):
    bsz, seq, _ = x.shape
    tm = _prompt_tile(seq)
    ns = seq // tm
    cq_tab, sq_tab, ck_tab, sk_tab = tabs
    tile = lambda w: pl.BlockSpec((1, tm, w), lambda s, b: (b, s, 0))
    in_specs = [
        tile(D_MODEL),
        pl.BlockSpec((6, 1, 1, D_MODEL), lambda s, b: (0, b, 0, 0)),
        _const_spec((1, D_MODEL)), _const_spec((1, D_MODEL)),
        _const_spec(lw["w_in"].shape), _const_spec((CONV_W, D_LRU)), _const_spec((1, D_LRU)),
        _const_spec(lw["w_gates"].shape), _const_spec((1, D_LRU)), _const_spec((1, D_LRU)),
        _const_spec((1, D_LRU)), _const_spec((1, Q_RANK)), _const_spec(lw["w_uq"].shape),
        _const_spec((1, KV_RANK)), _const_spec(lw["w_kv"].shape), _const_spec((1, D_LRU)),
        pl.BlockSpec((tm, D_HEADS_PAD), lambda s, b: (s, 0)),
        pl.BlockSpec((tm, D_HEADS_PAD), lambda s, b: (s, 0)),
        pl.BlockSpec((tm, LANES), lambda s, b: (s, 0)),
        pl.BlockSpec((tm, LANES), lambda s, b: (s, 0)),
    ]
    out_specs = [
        tile(D_HEADS_PAD), tile(D_HEADS_PAD), tile(D_HEADS_PAD), tile(D_LRU), tile(KV_RANK),
        tile(D_ROPE),
        pl.BlockSpec((1, 1, 1, D_LRU), lambda s, b: (s, b, 0, 0)),
        pl.BlockSpec((1, 1, CONV_W - 1, D_LRU), lambda s, b: (s, b, 0, 0)),
    ]
    out_shape = [
        jax.ShapeDtypeStruct((bsz, seq, D_HEADS_PAD), BF16),
        jax.ShapeDtypeStruct((bsz, seq, D_HEADS_PAD), BF16),
        jax.ShapeDtypeStruct((bsz, seq, D_HEADS_PAD), BF16),
        jax.ShapeDtypeStruct((bsz, seq, D_LRU), BF16),
        jax.ShapeDtypeStruct((bsz, seq, KV_RANK), F32),
        jax.ShapeDtypeStruct((bsz, seq, D_ROPE), F32),
        jax.ShapeDtypeStruct((ns, bsz, 1, D_LRU), F32),
        jax.ShapeDtypeStruct((ns, bsz, CONV_W - 1, D_LRU), F32),
    ]
    if apply_ln0:
        out_specs = [tile(D_MODEL)] + out_specs
        out_shape = [jax.ShapeDtypeStruct((bsz, seq, D_MODEL), F32)] + out_shape
    return pl.pallas_call(
        functools.partial(_prompt_pre_kernel, apply_ln0, tm),
        grid=(ns, bsz),
        in_specs=in_specs,
        out_specs=out_specs,
        out_shape=out_shape,
        scratch_shapes=[
            pltpu.VMEM((bsz, 1, D_LRU), F32),
            pltpu.VMEM((bsz, SUBLANES, D_LRU), F32),
            pltpu.VMEM((tm + SUBLANES, D_LRU), F32),
            pltpu.VMEM((tm, D_LRU), F32),
            pltpu.VMEM((tm, D_LRU), F32),
            pltpu.VMEM((tm, D_LRU), F32),
        ],
        compiler_params=pltpu.CompilerParams(
            dimension_semantics=("arbitrary", "arbitrary"), vmem_limit_bytes=VMEM_LIMIT),
        name="prompt_pre_ln0" if apply_ln0 else "prompt_pre",
    )(x, mod, ln0_g, ln0_b, lw["w_in"], lw["conv_w"], lw["conv_b"], lw["w_gates"], lw["b_a"],
      lw["b_x"], lw["lam"], lw["q_norm_g"], lw["w_uq"], lw["kv_norm_g"], lw["w_kv"],
      lw["g_lru"], cq_tab, sq_tab, ck_tab, sk_tab)


ATTN_HEADS_PER_STEP = 4


def _attn_kernel(tq, q_ref, k_ref, v_ref, o_ref):
    qi = pl.program_id(2)
    heads = range(ATTN_HEADS_PER_STEP)
    lanes = lambda hh: slice(hh * HEAD_PAD, (hh + 1) * HEAD_PAD)
    qs = [q_ref[0, :, lanes(hh)] for hh in heads]

    def tile_step(j, carry, masked):
        off = pl.multiple_of(j * tq, tq)
        out = []
        for hh in heads:
            m_prev, acc = carry[hh]
            kt = k_ref[0, pl.ds(off, tq), lanes(hh)]
            vt = v_ref[0, pl.ds(off, tq), lanes(hh)]
            s = lax.dot_general(qs[hh], kt, (((1,), (1,)), ((), ())),
                                preferred_element_type=F32)
            if masked:
                row = lax.broadcasted_iota(jnp.int32, (tq, tq), 0)
                col = lax.broadcasted_iota(jnp.int32, (tq, tq), 1)
                s = jnp.where(col <= row, s, -jnp.inf)
            m_new = jnp.maximum(m_prev, jnp.max(s, axis=1, keepdims=True))
            alpha = jnp.exp2((m_prev - m_new) * EXP2_SCALE)
            p = jnp.exp2((s - m_new) * EXP2_SCALE).astype(BF16)
            out.append((m_new, alpha * acc + _bdot(p, vt)))
        return tuple(out)

    init = tuple((jnp.full((tq, 1), -jnp.inf, F32), jnp.zeros((tq, HEAD_PAD), F32))
                 for _ in heads)
    carry = lax.fori_loop(0, qi, lambda j, c: tile_step(j, c, False), init)
    final = tile_step(qi, carry, True)
    value_lane = lax.broadcasted_iota(jnp.int32, (tq, HEAD_PAD), 1) < D_V
    for hh in heads:
        acc = final[hh][1]
        o_ref[0, :, lanes(hh)] = jnp.where(value_lane, acc / acc[:, D_V:D_V + 1], 0.0)


def _prompt_attention(q, k, v):
    bsz, seq, _ = q.shape
    tq = _prompt_tile(seq)
    width = ATTN_HEADS_PER_STEP * HEAD_PAD
    return pl.pallas_call(
        functools.partial(_attn_kernel, tq),
        grid=(bsz, N_HEADS // ATTN_HEADS_PER_STEP, seq // tq),
        in_specs=[
            pl.BlockSpec((1, tq, width), lambda b, h, i: (b, i, h)),
            pl.BlockSpec((1, seq, width), lambda b, h, i: (b, 0, h)),
            pl.BlockSpec((1, seq, width), lambda b, h, i: (b, 0, h)),
        ],
        out_specs=pl.BlockSpec((1, tq, width), lambda b, h, i: (b, i, h)),
        out_shape=jax.ShapeDtypeStruct((bsz, seq, D_HEADS_PAD), F32),
        compiler_params=pltpu.CompilerParams(
            dimension_semantics=("arbitrary", "arbitrary", "arbitrary"),
            vmem_limit_bytes=VMEM_LIMIT),
        name="prompt_attention",
    )(q, k, v)


def _post_kernel(from_latent, ff_chunk, x_ref, mod_ref, mixl_ref, att_ref, wuv_ref, gatt_ref,
                 wol_ref, woa_ref, lng_ref, lnb_ref, wup_ref, wdn_ref, o_ref):
    x = x_ref[0]
    if from_latent:
        ya = jnp.concatenate(
            [_bdot(att_ref[hh], wuv_ref[hh]) for hh in range(N_HEADS)], axis=1)
    else:
        ya = att_ref[0]
    na = _rms_norm(ya, gatt_ref[...], N_HEADS * D_V).astype(BF16)
    o = _bdot(mixl_ref[0], wol_ref[...]) + _bdot(na, woa_ref[...])
    x1 = _layer_norm(ALPHA * x + mod_ref[2, 0] * o, lng_ref[0:1, :], lnb_ref[0:1, :])
    h2 = (x1 * (1.0 + mod_ref[4, 0]) + mod_ref[3, 0]).astype(BF16)
    f = jnp.zeros(x.shape, F32)
    for c in range(D_FF // ff_chunk):
        up = _bdot(h2, wup_ref[:, c * ff_chunk:(c + 1) * ff_chunk])
        act = jnp.square(jnp.maximum(up, 0.0)).astype(BF16)
        f = f + _bdot(act, wdn_ref[c * ff_chunk:(c + 1) * ff_chunk, :])
    o_ref[0] = _layer_norm(ALPHA * x1 + mod_ref[5, 0] * f, lng_ref[1:2, :], lnb_ref[1:2, :])


def _post(from_latent, x, mod, mixl, att, lw):
    groups, rows, _ = x.shape
    tm = _prompt_tile(rows)
    tile = lambda w: pl.BlockSpec((1, tm, w), lambda g, s: (g, s, 0))
    mod_rows = mod.shape[2]
    if from_latent:
        att_spec = pl.BlockSpec((N_HEADS, tm, KV_RANK), lambda g, s: (0, s, 0))
    else:
        att_spec = tile(D_HEADS_PAD)
    return pl.pallas_call(
        functools.partial(_post_kernel, from_latent, 1024),
        grid=(groups, rows // tm),
        in_specs=[
            tile(D_MODEL),
            pl.BlockSpec((6, 1, mod_rows, D_MODEL), lambda g, s: (0, g, 0, 0)),
            tile(D_LRU), att_spec,
            _const_spec(lw["w_uv_heads"].shape), _const_spec((1, D_HEADS_PAD)),
            _const_spec(lw["w_o_lru"].shape), _const_spec(lw["w_o_att"].shape),
            _const_spec((2, D_MODEL)), _const_spec((2, D_MODEL)),
            _const_spec(lw["w_up"].shape), _const_spec(lw["w_down"].shape),
        ],
        out_specs=tile(D_MODEL),
        out_shape=jax.ShapeDtypeStruct(x.shape, F32),
        compiler_params=pltpu.CompilerParams(
            dimension_semantics=("arbitrary", "arbitrary"), vmem_limit_bytes=VMEM_LIMIT),
        name="sample_post" if from_latent else "prompt_post",
    )(x, mod, mixl, att, lw["w_uv_heads"], lw["g_att"], lw["w_o_lru"], lw["w_o_att"],
      lw["ln_g"], lw["ln_b"], lw["w_up"], lw["w_down"])


def _sample_pre_kernel(apply_ln0, *refs):
    (x_ref, mod_ref, ln0g_ref, ln0b_ref, win_ref, convw_ref, convb_ref, wg_ref, ba_ref,
     bx_ref, lam_ref, qg_ref, wuq_ref, kvg_ref, wukt_ref, glru_ref,
     cq_ref, sq_ref, ck_ref, sk_ref, h0_ref, buf_ref) = refs[:22]
    outs = refs[22:]
    if apply_ln0:
        xn_ref, outs = outs[0], outs[1:]
    q_ref, qlat_ref, mixl_ref, ckv_ref, kr_ref, hnew_ref, conv_ref = outs

    x = x_ref[...]
    if apply_ln0:
        x = _layer_norm(x, ln0g_ref[...], ln0b_ref[...])
        xn_ref[...] = x
    h = (x * (1.0 + mod_ref[1, 0]) + mod_ref[0, 0]).astype(BF16)
    proj = _bdot(h, win_ref[...])
    u_raw = proj[:, OFF_U:OFF_GATE]
    gate = proj[:, OFF_GATE:OFF_CQ]

    uc = convb_ref[...] + u_raw * convw_ref[CONV_W - 1:CONV_W, :]
    for kk in range(CONV_W - 1):
        uc = uc + buf_ref[kk] * convw_ref[kk:kk + 1, :]
    for kk in range(CONV_W - 2):
        conv_ref[kk] = buf_ref[kk + 1]
    conv_ref[CONV_W - 2] = u_raw

    a, bb = _lru_coeffs(uc, wg_ref, ba_ref, bx_ref, lam_ref)
    h_new = a * h0_ref[...] + bb
    hnew_ref[...] = h_new
    y = h_new * _gelu_tanh(gate)
    mixl_ref[...] = _rms_norm(y, glru_ref[...], D_LRU).astype(BF16)

    q, ckvn, krw = _latent_heads(proj, cq_ref[0:1, :], sq_ref[0:1, :], ck_ref[0:1, :],
                                 sk_ref[0:1, :], qg_ref, wuq_ref, kvg_ref)
    qb = q.astype(BF16)
    q_ref[...] = qb
    ckv_ref[...] = ckvn
    kr_ref[...] = krw[:, :D_ROPE]
    for hh in range(N_HEADS):
        qlat_ref[hh] = _bdot(qb[:, hh * HEAD_PAD:(hh + 1) * HEAD_PAD], wukt_ref[hh])


def _sample_pre(apply_ln0, x, mod, ln0_g, ln0_b, lw, tabs, h0, buf):
    rows = x.shape[0]
    cq_tab, sq_tab, ck_tab, sk_tab = tabs
    full = lambda a: _const_spec(a.shape)
    args = (x, mod, ln0_g, ln0_b, lw["w_in"], lw["conv_w"], lw["conv_b"], lw["w_gates"],
            lw["b_a"], lw["b_x"], lw["lam"], lw["q_norm_g"], lw["w_uq"], lw["kv_norm_g"],
            lw["w_uk_t"], lw["g_lru"], cq_tab, sq_tab, ck_tab, sk_tab, h0, buf)
    out_shape = [
        jax.ShapeDtypeStruct((rows, D_HEADS_PAD), BF16),
        jax.ShapeDtypeStruct((N_HEADS, rows, KV_RANK), F32),
        jax.ShapeDtypeStruct((rows, D_LRU), BF16),
        jax.ShapeDtypeStruct((rows, KV_RANK), F32),
        jax.ShapeDtypeStruct((rows, D_ROPE), F32),
        jax.ShapeDtypeStruct((rows, D_LRU), F32),
        jax.ShapeDtypeStruct((CONV_W - 1, rows, D_LRU), F32),
    ]
    if apply_ln0:
        out_shape = [jax.ShapeDtypeStruct((rows, D_MODEL), F32)] + out_shape
    return pl.pallas_call(
        functools.partial(_sample_pre_kernel, apply_ln0),
        grid=(1,),
        in_specs=[full(a) for a in args],
        out_specs=[_const_spec(s.shape) for s in out_shape],
        out_shape=out_shape,
        compiler_params=pltpu.CompilerParams(
            dimension_semantics=("arbitrary",), vmem_limit_bytes=VMEM_LIMIT),
        name="sample_pre_ln0" if apply_ln0 else "sample_pre",
    )(*args)


PAGED_SLOTS = 4


def _paged_attn_kernel(layer, pages_per_chunk, chunks_per_seq, page_size,
                       pt_ref, qlat_ref, qrope_ref, cnew_ref, knew_ref, cache_c, cache_kt,
                       o_ref, *scratch):
    cbufs = scratch[:PAGED_SLOTS]
    kbufs = scratch[PAGED_SLOTS:2 * PAGED_SLOTS]
    sem_c, sem_k, m_sc, l_sc, acc_sc = scratch[2 * PAGED_SLOTS:]
    g = pl.program_id(0)
    n_steps = pl.num_programs(0)
    total_chunks = PAGED_SLOTS * n_steps
    bufs = tuple(zip(cbufs, kbufs))
    ahead = PAGED_SLOTS - 1

    def page_copies(chunk_idx, sl, j):
        seq = chunk_idx // chunks_per_seq
        page = pt_ref[seq, (chunk_idx % chunks_per_seq) * pages_per_chunk + j]
        span = pl.ds(j * page_size, page_size)
        cb, kb = bufs[sl]
        return (pltpu.make_async_copy(cache_c.at[layer, page], cb.at[span, :], sem_c.at[sl]),
                pltpu.make_async_copy(cache_kt.at[layer, page], kb.at[:, span], sem_k.at[sl]))

    def start_chunk(chunk_idx, sl):
        for j in range(pages_per_chunk):
            for cp in page_copies(chunk_idx, sl, j):
                cp.start()

    def wait_chunk(chunk_idx, sl):
        for j in range(pages_per_chunk):
            for cp in page_copies(chunk_idx, sl, j):
                cp.wait()

    qlat = qlat_ref[0]
    qrope = qrope_ref[0]

    def consume(sl, m_prev, l_prev, acc_prev):
        cb, kb = bufs[sl]
        ck = cb[...].astype(BF16)
        s = (lax.dot_general(qlat, ck, (((1,), (1,)), ((), ())), preferred_element_type=F32)
             + _bdot(qrope, kb[...].astype(BF16)))
        m_new = jnp.maximum(m_prev, jnp.max(s, axis=1, keepdims=True))
        alpha = jnp.exp2((m_prev - m_new) * EXP2_SCALE)
        p = jnp.exp2((s - m_new) * EXP2_SCALE)
        l_new = alpha * l_prev + jnp.sum(p, axis=1, keepdims=True)
        acc = alpha * acc_prev + _bdot(p.astype(BF16), ck)
        return m_new, l_new, acc

    first = PAGED_SLOTS * g

    @pl.when(g == 0)
    def _():
        for k in range(ahead):
            start_chunk(k, k)

    cnew = cnew_ref[0].astype(BF16).astype(F32)
    knew = knew_ref[0].astype(BF16).astype(F32)
    s_new = (jnp.sum(qlat.astype(F32) * cnew, axis=1, keepdims=True)
             + jnp.sum(qrope.astype(F32) * knew, axis=1, keepdims=True))
    opens = lax.rem(first, chunks_per_seq) == 0
    state = (jnp.where(opens, s_new, m_sc[...]),
             jnp.where(opens, jnp.ones_like(s_new), l_sc[...]),
             jnp.where(opens, jnp.broadcast_to(cnew, acc_sc.shape), acc_sc[...]))

    for k in range(PAGED_SLOTS):
        wait_chunk(first + k, k)
        state = consume(k, *state)
        start_chunk(lax.rem(first + k + ahead, total_chunks), (k + ahead) % PAGED_SLOTS)

    m_fin, l_fin, acc_fin = state
    m_sc[...] = m_fin
    l_sc[...] = l_fin
    acc_sc[...] = acc_fin
    o_ref[0] = acc_fin / l_fin

    @pl.when(g == n_steps - 1)
    def _():
        for k in range(ahead):
            wait_chunk(k, k)


def _paged_pages_per_chunk(n_pages):
    for p in (32, 16, 8, 4, 2, 1):
        if n_pages % (PAGED_SLOTS * p) == 0:
            return p
    raise ValueError(f"the page count per sequence ({n_pages}) must be a multiple of {PAGED_SLOTS}")


def _paged_attention(layer, page_table, qlat, qrope, cnew, knew, cache_ckv, cache_krope_t):
    nseq, n_pages = page_table.shape
    page_size = cache_ckv.shape[2]
    ppc = _paged_pages_per_chunk(n_pages)
    chunks_per_seq = n_pages // ppc
    steps_per_seq = chunks_per_seq // PAGED_SLOTS
    per_seq = lambda w: pl.BlockSpec((1, N_HEADS, w), lambda s, pt: (s // steps_per_seq, 0, 0))
    one_row = lambda w: pl.BlockSpec((1, 1, w), lambda s, pt: (s // steps_per_seq, 0, 0))
    grid_spec = pltpu.PrefetchScalarGridSpec(
        num_scalar_prefetch=1,
        grid=(nseq * steps_per_seq,),
        in_specs=[
            per_seq(KV_RANK), per_seq(D_ROPE), one_row(KV_RANK), one_row(D_ROPE),
            pl.BlockSpec(memory_space=pl.ANY), pl.BlockSpec(memory_space=pl.ANY),
        ],
        out_specs=per_seq(KV_RANK),
        scratch_shapes=(
            [pltpu.VMEM((ppc * page_size, KV_RANK), F32)] * PAGED_SLOTS
            + [pltpu.VMEM((D_ROPE, ppc * page_size), F32)] * PAGED_SLOTS
        ) + [
            pltpu.SemaphoreType.DMA((PAGED_SLOTS,)),
            pltpu.SemaphoreType.DMA((PAGED_SLOTS,)),
            pltpu.VMEM((N_HEADS, 1), F32),
            pltpu.VMEM((N_HEADS, 1), F32),
            pltpu.VMEM((N_HEADS, KV_RANK), F32),
        ],
    )
    return pl.pallas_call(
        functools.partial(_paged_attn_kernel, layer, ppc, chunks_per_seq, page_size),
        grid_spec=grid_spec,
        out_shape=jax.ShapeDtypeStruct((nseq, N_HEADS, KV_RANK), F32),
        compiler_params=pltpu.CompilerParams(
            dimension_semantics=("arbitrary",), vmem_limit_bytes=VMEM_LIMIT),
        name="paged_attention",
    )(page_table, qlat, qrope, cnew, knew, cache_ckv, cache_krope_t)


def _pad_heads(w, width):
    pad = [(0, 0)] * (w.ndim - 1) + [(0, HEAD_PAD - width)]
    return jnp.pad(w, pad).reshape(*w.shape[:-2], D_HEADS_PAD)


def _block_diag(w):
    eye = jnp.eye(LRU_BLOCKS, dtype=w.dtype)
    return jnp.einsum("nij,nm->nimj", w, eye).reshape(D_LRU, D_LRU)


def _layer_weights(l, p):
    w_in = p["w_in"][l]
    kr_cols = w_in[:, OFF_KRA:OFF_KRA + D_ROPE]
    kr_swap = jnp.concatenate([-kr_cols[:, HALF_ROPE:], kr_cols[:, :HALF_ROPE]], axis=1)
    lane_pad = lambda w: jnp.pad(w, ((0, 0), (0, LANES - w.shape[1])))
    w_in_pad = jnp.concatenate([w_in[:, :OFF_KRA], lane_pad(kr_cols), lane_pad(kr_swap)], axis=1)

    w_uq = p["w_uq"][l]
    x1 = w_uq[..., D_NOPE:D_NOPE + HALF_ROPE]
    x2 = w_uq[..., D_NOPE + HALF_ROPE:]
    w_uq_swap = jnp.concatenate([jnp.zeros_like(w_uq[..., :D_NOPE]), -x2, x1], axis=-1)
    w_uq_pad = jnp.concatenate([_pad_heads(w_uq, D_QK), _pad_heads(w_uq_swap, D_QK)], axis=1)

    w_uk = p["w_uk"][l]
    w_uv = p["w_uv"][l]
    place = np.zeros((LANES, N_HEADS, HEAD_PAD), np.float32)
    for j in range(D_ROPE):
        place[j, :, D_NOPE + j] = 1.0
    w_k = jnp.concatenate([_pad_heads(w_uk, D_NOPE), jnp.asarray(place.reshape(LANES, -1))], axis=0)
    w_v = jnp.concatenate([_pad_heads(w_uv, D_V), jnp.zeros((LANES, D_HEADS_PAD), F32)], axis=0)
    w_kv = jnp.concatenate([w_k, w_v], axis=1)

    w_uk_t = jnp.pad(jnp.transpose(w_uk, (1, 2, 0)), ((0, 0), (0, HEAD_PAD - D_NOPE), (0, 0)))
    w_uv_heads = jnp.pad(jnp.transpose(w_uv, (1, 0, 2)), ((0, 0), (0, 0), (0, HEAD_PAD - D_V)))

    w_o = p["w_o"][l]
    w_o_att = jnp.pad(w_o[D_LRU:].reshape(N_HEADS, D_V, D_MODEL),
                      ((0, 0), (0, HEAD_PAD - D_V), (0, 0))).reshape(D_HEADS_PAD, D_MODEL)
    g_att = _pad_heads(p["g_att"][l].reshape(1, N_HEADS, D_V), D_V)
    row = lambda v: v.reshape(1, -1)
    return {
        "w_in": w_in_pad.astype(BF16),
        "conv_w": p["conv_w"][l], "conv_b": row(p["conv_b"][l]),
        "w_gates": jnp.concatenate([_block_diag(p["w_a"][l]), _block_diag(p["w_x"][l])],
                                   axis=1).astype(BF16),
        "b_a": row(p["b_a"][l]), "b_x": row(p["b_x"][l]), "lam": row(p["lru_lambda"][l]),
        "q_norm_g": row(p["q_norm_g"][l]), "w_uq": w_uq_pad.astype(BF16),
        "kv_norm_g": row(p["kv_norm_g"][l]), "w_kv": w_kv.astype(BF16),
        "w_uk_t": w_uk_t.astype(BF16), "w_uv_heads": w_uv_heads.astype(BF16),
        "g_lru": row(p["g_lru"][l]), "g_att": g_att,
        "w_o_lru": w_o[:D_LRU].astype(BF16), "w_o_att": w_o_att.astype(BF16),
        "ln_g": p["ln_g"][l], "ln_b": p["ln_b"][l],
        "w_up": p["w_up"][l].astype(BF16), "w_down": p["w_down"][l].astype(BF16),
    }


def kernel(x_prompt, x_sample, cache_ckv, cache_krope, state_lru_h, state_conv, page_table,
           c_prompt, c_sample, ln0_g, ln0_b, w_ada, b_ada, w_in, conv_w, conv_b, w_a, b_a,
           w_x, b_x, lru_lambda, q_norm_g, w_uq, kv_norm_g, w_uk, w_uv, g_lru, g_att, w_o,
           ln_g, ln_b, w_up, w_down):
    params = dict(w_in=w_in, conv_w=conv_w, conv_b=conv_b, w_a=w_a, b_a=b_a, w_x=w_x, b_x=b_x,
                  lru_lambda=lru_lambda, q_norm_g=q_norm_g, w_uq=w_uq, kv_norm_g=kv_norm_g,
                  w_uk=w_uk, w_uv=w_uv, g_lru=g_lru, g_att=g_att, w_o=w_o, ln_g=ln_g,
                  ln_b=ln_b, w_up=w_up, w_down=w_down)
    b_p, s_p, _ = x_prompt.shape
    b_s, s_s, _ = x_sample.shape
    if s_s != 1:
        raise ValueError("the sample group is a single-token decode step")
    depth = w_ada.shape[0]
    past_len = page_table.shape[1] * cache_ckv.shape[2]

    mods = _ada_mod(jnp.concatenate([c_prompt, c_sample], axis=0), w_ada, b_ada)
    mods = mods.reshape(depth, b_p + b_s, 6, D_MODEL)
    q_rows, k_rows = _rope_lane_rows()
    tabs_p = _rope_tables(s_p, 0, N_HEADS, *q_rows) + _rope_tables(s_p, 0, 1, *k_rows)
    tabs_s = (_rope_tables(SUBLANES, past_len, N_HEADS, *q_rows)
              + _rope_tables(SUBLANES, past_len, 1, *k_rows))
    ln0g, ln0b = ln0_g.reshape(1, -1), ln0_b.reshape(1, -1)
    cache_krope_t = jnp.swapaxes(cache_krope, 2, 3)

    xp = x_prompt
    xs = x_sample.reshape(b_s, D_MODEL)
    outs_p = [[] for _ in range(4)]
    outs_s = [[] for _ in range(4)]
    for l in range(depth):
        lw = _layer_weights(l, params)
        first = l == 0
        mod_p = jnp.transpose(mods[l, :b_p], (1, 0, 2)).reshape(6, b_p, 1, D_MODEL)
        mod_s = jnp.transpose(mods[l, b_p:], (1, 0, 2)).reshape(6, 1, b_s, D_MODEL)

        res = _prompt_pre(first, xp, mod_p, ln0g, ln0b, lw, tabs_p)
        if first:
            xp, res = res[0], res[1:]
        q, k, v, mixl, ckv, kr, h_last, conv = res
        att = _prompt_attention(q, k, v)
        xp = _post(False, xp, mod_p, mixl, att, lw)
        for dst, val in zip(outs_p, (ckv, kr, h_last[-1].reshape(b_p, D_LRU), conv[-1])):
            dst.append(val)

        res = _sample_pre(first, xs, mod_s, ln0g, ln0b, lw, tabs_s, state_lru_h[l],
                          jnp.transpose(state_conv[l], (1, 0, 2)))
        if first:
            xs, res = res[0], res[1:]
        q_s, qlat, mixl_s, ckv_s, kr_s, h_new, conv_s = res
        qlat_b = jnp.transpose(qlat, (1, 0, 2)).astype(BF16)
        qrope_b = q_s.reshape(b_s, N_HEADS, HEAD_PAD)[:, :, D_NOPE:D_QK]
        o_lat = _paged_attention(l, page_table, qlat_b, qrope_b, ckv_s.reshape(b_s, 1, KV_RANK),
                                 kr_s.reshape(b_s, 1, D_ROPE), cache_ckv, cache_krope_t)
        o_lat_h = jnp.transpose(o_lat, (1, 0, 2)).astype(BF16)
        xs = _post(True, xs.reshape(1, b_s, D_MODEL), mod_s, mixl_s.reshape(1, b_s, D_LRU),
                   o_lat_h, lw).reshape(b_s, D_MODEL)
        for dst, val in zip(outs_s, (ckv_s.reshape(b_s, 1, KV_RANK), kr_s.reshape(b_s, 1, D_ROPE),
                                     h_new, jnp.transpose(conv_s, (1, 0, 2)))):
            dst.append(val)

    stack = lambda vals: jnp.stack(vals, axis=0)
    return (xp, xs.reshape(b_s, 1, D_MODEL),
            stack(outs_p[0]), stack(outs_p[1]), stack(outs_p[2]), stack(outs_p[3]),
            stack(outs_s[0]), stack(outs_s[1]), stack(outs_s[2]), stack(outs_s[3]))
```

```python
import functools
import math

import jax
import jax.numpy as jnp
import numpy as np
from jax import lax
from jax.experimental import pallas as pl
from jax.experimental.pallas import tpu as pltpu

D_MODEL = 1024
D_LRU = 512
LRU_BLOCKS = 8
LRU_BLOCK = D_LRU // LRU_BLOCKS
CONV_W = 4
LRU_C = 8.0
N_HEADS = 8
D_NOPE = 64
D_ROPE = 32
D_QK = D_NOPE + D_ROPE
D_V = 64
Q_RANK = 384
KV_RANK = 256
ROPE_THETA = 10000.0
D_FF = 4 * D_MODEL
DEPTH = 2
ALPHA = (2 * DEPTH) ** 0.25
ATT_SCALE = D_QK ** -0.5
EPS = 1e-6

LANES = 128
SUBLANES = 8
VMEM_BYTES_V7X = 64 * 1024 * 1024
VMEM_LIMIT = VMEM_BYTES_V7X - 8 * 1024 * 1024

HEAD_PAD = LANES
D_HEADS_PAD = N_HEADS * HEAD_PAD
HALF_ROPE = D_ROPE // 2
OFF_U, OFF_GATE, OFF_CQ, OFF_CKV = 0, D_LRU, 2 * D_LRU, 2 * D_LRU + Q_RANK
OFF_KRA = OFF_CKV + KV_RANK
OFF_KRB = OFF_KRA + LANES
D_IN_PAD = OFF_KRB + LANES
EXP2_SCALE = ATT_SCALE * math.log2(math.e)
GATE_CHUNK = 256
OFF_KR_LANE = D_NOPE

F32 = jnp.float32
BF16 = jnp.bfloat16


def _prompt_tile(seq):
    for t in (512, 256, 128, 64, 32, 16, 8):
        if seq % t == 0:
            return t
    raise ValueError(f"sequence length {seq} must be a multiple of 8")


def _const_spec(shape):
    nd = len(shape)
    return pl.BlockSpec(shape, lambda *_: (0,) * nd, pipeline_mode=pl.Buffered(1))


def _layer_norm(x, g, b):
    mu = jnp.mean(x, axis=-1, keepdims=True)
    xc = x - mu
    var = jnp.mean(xc * xc, axis=-1, keepdims=True)
    return xc * lax.rsqrt(var + EPS) * g + b


def _rms_norm(x, g, n):
    ms = jnp.sum(x * x, axis=-1, keepdims=True) * (1.0 / n)
    return x * lax.rsqrt(ms + EPS) * g


def _sigmoid(x):
    return 1.0 / (1.0 + jnp.exp(-x))


def _softplus(x):
    return jnp.maximum(x, 0.0) + jnp.log1p(jnp.exp(-jnp.abs(x)))


def _gelu_tanh(x):
    c = math.sqrt(2.0 / math.pi)
    return 0.5 * x * (1.0 + jnp.tanh(c * (x + 0.044715 * (x * x * x))))


def _bdot(a, b):
    return jnp.dot(a, b, preferred_element_type=F32)


def _lru_coeffs(uc, wg_ref, ba_ref, bx_ref, lam_ref):
    ub = uc.astype(BF16)
    chunks = [ub[:, c * GATE_CHUNK:(c + 1) * GATE_CHUNK] for c in range(D_LRU // GATE_CHUNK)]
    gate = lambda n: jnp.concatenate(
        [_bdot(x, wg_ref[n, c]) for c, x in enumerate(chunks)], axis=1)
    r = _sigmoid(gate(0) + ba_ref[...])
    i = _sigmoid(gate(1) + bx_ref[...])
    log_a = (-LRU_C) * r * _softplus(-lam_ref[...])
    a = jnp.exp(log_a)
    t = jnp.tanh(log_a)
    b = jnp.sqrt((-2.0 * t) / (1.0 - t)) * (i * uc)
    return a, b


def _latent_heads(proj, cq_tab, sq_tab, ck_tab, sk_tab, qg_ref, wuq_ref, kvg_ref):
    cqn = _rms_norm(proj[:, OFF_CQ:OFF_CKV], qg_ref[...], Q_RANK)
    qq = _bdot(cqn.astype(BF16), wuq_ref[...])
    q = qq[:, :D_HEADS_PAD] * cq_tab + qq[:, D_HEADS_PAD:] * sq_tab
    ckvn = _rms_norm(proj[:, OFF_CKV:OFF_KRA], kvg_ref[...], KV_RANK)
    krw = proj[:, OFF_KRA:OFF_KRB] * ck_tab + proj[:, OFF_KRB:D_IN_PAD] * sk_tab
    return q, ckvn, krw


def _ada_kernel(c_ref, w_ref, b_ref, o_ref):
    c = c_ref[...]
    s = (c * _sigmoid(c)).astype(BF16)
    o_ref[0, 0] = _bdot(s, w_ref[0].astype(BF16)) + b_ref[0]


def _ada_mod(c_all, w_ada, b_ada):
    rows = c_all.shape[0]
    depth, _, n_out = w_ada.shape
    n_mod = n_out // D_MODEL
    return pl.pallas_call(
        _ada_kernel,
        grid=(depth, n_mod),
        in_specs=[
            pl.BlockSpec((rows, D_MODEL), lambda l, j: (0, 0)),
            pl.BlockSpec((1, D_MODEL, D_MODEL), lambda l, j: (l, 0, j)),
            pl.BlockSpec((1, 1, D_MODEL), lambda l, j: (l, 0, j)),
        ],
        out_specs=pl.BlockSpec((1, 1, rows, D_MODEL), lambda l, j: (l, j, 0, 0)),
        out_shape=jax.ShapeDtypeStruct((depth, n_mod, rows, D_MODEL), F32),
        compiler_params=pltpu.CompilerParams(
            dimension_semantics=("arbitrary", "arbitrary"), vmem_limit_bytes=VMEM_LIMIT),
        name="ada_mod",
    )(c_all, w_ada, b_ada.reshape(depth, 1, n_out))


def _rope_table_kernel(pos0, inv_ref, one_ref, rope_ref, cq_ref, sq_ref, ck_ref, sk_ref):
    rows = ck_ref.shape[0]
    pos = (lax.broadcasted_iota(jnp.int32, (rows, 1), 0)
           + (pl.program_id(0) * rows + pos0)).astype(F32)
    ang = pos * inv_ref[...]
    cos = rope_ref[...] * jnp.cos(ang)
    sin = rope_ref[...] * jnp.sin(ang)
    ck_ref[...] = cos
    sk_ref[...] = sin
    cq_ref[...] = jnp.concatenate([(cos + one_ref[...]) * EXP2_SCALE] * N_HEADS, axis=1)
    sq_ref[...] = jnp.concatenate([sin * EXP2_SCALE] * N_HEADS, axis=1)


def _rope_tables(rows, pos0):
    inv = ROPE_THETA ** (-jnp.arange(HALF_ROPE, dtype=F32) / HALF_ROPE)
    zeros = lambda n: jnp.zeros((n,), F32)
    ones = lambda n: jnp.ones((n,), F32)
    head_inv = jnp.concatenate([zeros(D_NOPE), inv, inv, zeros(HEAD_PAD - D_QK)])[None, :]
    head_rope = jnp.concatenate([zeros(D_NOPE), ones(D_ROPE), zeros(HEAD_PAD - D_QK)])[None, :]
    head_one = jnp.concatenate([ones(D_NOPE), zeros(HEAD_PAD - D_NOPE)])[None, :]
    tr = _prompt_tile(rows)
    widths = (D_HEADS_PAD, D_HEADS_PAD, LANES, LANES)
    return pl.pallas_call(
        functools.partial(_rope_table_kernel, pos0),
        grid=(rows // tr,),
        in_specs=[_const_spec((1, LANES))] * 3,
        out_specs=[pl.BlockSpec((tr, w), lambda i: (i, 0)) for w in widths],
        out_shape=[jax.ShapeDtypeStruct((rows, w), F32) for w in widths],
        compiler_params=pltpu.CompilerParams(dimension_semantics=("arbitrary",)),
        name="rope_tables",
    )(head_inv, head_one, head_rope)


def _prompt_pre_kernel(apply_ln0, tm, *refs):
    (x_ref, mod_ref, ln0g_ref, ln0b_ref, win_ref, convw_ref, convb_ref, wg_ref, ba_ref,
     bx_ref, lam_ref, qg_ref, wuq_ref, kvg_ref, wkv_ref, glru_ref,
     cq_ref, sq_ref, ck_ref, sk_ref) = refs[:20]
    outs = refs[20:]
    if apply_ln0:
        xn_ref, outs = outs[0], outs[1:]
    (q_ref, k_ref, v_ref, mixl_ref, ckv_ref, kr_ref, hlast_ref, conv_ref,
     hcar_ref, halo_ref, ubuf_ref, a_sc, b_sc, hs_sc) = outs

    s_idx = pl.program_id(0)
    b_idx = pl.program_id(1)

    x = x_ref[0]
    if apply_ln0:
        x = _layer_norm(x, ln0g_ref[...], ln0b_ref[...])
        xn_ref[0] = x
    h = (x * (1.0 + mod_ref[1, 0]) + mod_ref[0, 0]).astype(BF16)
    proj = _bdot(h, win_ref[...])
    u_raw = proj[:, OFF_U:OFF_GATE]
    gate = proj[:, OFF_GATE:OFF_CQ]

    @pl.when(s_idx == 0)
    def _():
        halo_ref[b_idx] = jnp.zeros((SUBLANES, D_LRU), F32)
        hcar_ref[b_idx] = jnp.zeros((1, D_LRU), F32)

    ubuf_ref[0:SUBLANES, :] = halo_ref[b_idx]
    ubuf_ref[SUBLANES:, :] = u_raw
    halo_ref[b_idx] = u_raw[tm - SUBLANES:, :]
    uc = convb_ref[...] + u_raw * convw_ref[CONV_W - 1:CONV_W, :]
    for kk in range(CONV_W - 1):
        off = SUBLANES - (CONV_W - 1) + kk
        uc = uc + ubuf_ref[off:off + tm, :] * convw_ref[kk:kk + 1, :]
    conv_ref[0, 0] = ubuf_ref[tm + SUBLANES - (CONV_W - 1):tm + SUBLANES, :]

    a, bb = _lru_coeffs(uc, wg_ref, ba_ref, bx_ref, lam_ref)

    groups = (tm // SUBLANES, SUBLANES, D_LRU)
    a = a.reshape(groups)
    bb = bb.reshape(groups)
    row8 = lax.broadcasted_iota(jnp.int32, (1, SUBLANES, D_LRU), 1)
    for d in (1, 2, 4):
        a_sh = pltpu.roll(a, d, axis=1)
        b_sh = pltpu.roll(bb, d, axis=1)
        take = row8 >= d
        bb = jnp.where(take, a * b_sh + bb, bb)
        a = jnp.where(take, a * a_sh, a)
    a_sc[...] = a.reshape(tm, D_LRU)
    b_sc[...] = bb.reshape(tm, D_LRU)

    def group_step(g, hprev):
        off = pl.multiple_of(g * SUBLANES, SUBLANES)
        hg = a_sc[pl.ds(off, SUBLANES), :] * hprev + b_sc[pl.ds(off, SUBLANES), :]
        hs_sc[pl.ds(off, SUBLANES), :] = hg
        return hg[SUBLANES - 1:SUBLANES, :]

    h_fin = lax.fori_loop(0, tm // SUBLANES, group_step, hcar_ref[b_idx], unroll=8)
    hcar_ref[b_idx] = h_fin
    hlast_ref[0, 0] = h_fin

    y = hs_sc[...] * _gelu_tanh(gate)
    mixl_ref[0] = _rms_norm(y, glru_ref[...], D_LRU).astype(BF16)

    q, ckvn, krw = _latent_heads(proj, cq_ref[...], sq_ref[...], ck_ref[...], sk_ref[...],
                                 qg_ref, wuq_ref, kvg_ref)
    q_ref[0] = q.astype(BF16)
    ckv_ref[0] = ckvn
    kr_ref[0] = krw[:, OFF_KR_LANE:OFF_KR_LANE + D_ROPE]
    kv = _bdot(ckvn.astype(BF16), wkv_ref[...])
    k_ref[0] = (kv[:, :D_HEADS_PAD] + jnp.concatenate([krw] * N_HEADS, axis=1)).astype(BF16)
    lane = lax.broadcasted_iota(jnp.int32, (1, D_HEADS_PAD), 1) & (HEAD_PAD - 1)
    v_ref[0] = (kv[:, D_HEADS_PAD:] + (lane == D_V).astype(F32)).astype(BF16)


def _prompt_pre(apply_ln0, x, mod, ln0_g, ln0_b, lw, tabs):
    bsz, seq, _ = x.shape
    tm = _prompt_tile(seq)
    ns = seq // tm
    cq_tab, sq_tab, ck_tab, sk_tab = tabs
    tile = lambda w: pl.BlockSpec((1, tm, w), lambda s, b: (b, s, 0))
    in_specs = [
        tile(D_MODEL),
        pl.BlockSpec((6, 1, 1, D_MODEL), lambda s, b: (0, b, 0, 0)),
        _const_spec((1, D_MODEL)), _const_spec((1, D_MODEL)),
        _const_spec(lw["w_in"].shape), _const_spec((CONV_W, D_LRU)), _const_spec((1, D_LRU)),
        _const_spec(lw["w_gates"].shape), _const_spec((1, D_LRU)), _const_spec((1, D_LRU)),
        _const_spec((1, D_LRU)), _const_spec((1, Q_RANK)), _const_spec(lw["w_uq"].shape),
        _const_spec((1, KV_RANK)), _const_spec(lw["w_kv"].shape), _const_spec((1, D_LRU)),
        pl.BlockSpec((tm, D_HEADS_PAD), lambda s, b: (s, 0)),
        pl.BlockSpec((tm, D_HEADS_PAD), lambda s, b: (s, 0)),
        pl.BlockSpec((tm, LANES), lambda s, b: (s, 0)),
        pl.BlockSpec((tm, LANES), lambda s, b: (s, 0)),
    ]
    out_specs = [
        tile(D_HEADS_PAD), tile(D_HEADS_PAD), tile(D_HEADS_PAD), tile(D_LRU), tile(KV_RANK),
        tile(D_ROPE),
        pl.BlockSpec((1, 1, 1, D_LRU), lambda s, b: (s, b, 0, 0)),
        pl.BlockSpec((1, 1, CONV_W - 1, D_LRU), lambda s, b: (s, b, 0, 0)),
    ]
    out_shape = [
        jax.ShapeDtypeStruct((bsz, seq, D_HEADS_PAD), BF16),
        jax.ShapeDtypeStruct((bsz, seq, D_HEADS_PAD), BF16),
        jax.ShapeDtypeStruct((bsz, seq, D_HEADS_PAD), BF16),
        jax.ShapeDtypeStruct((bsz, seq, D_LRU), BF16),
        jax.ShapeDtypeStruct((bsz, seq, KV_RANK), F32),
        jax.ShapeDtypeStruct((bsz, seq, D_ROPE), F32),
        jax.ShapeDtypeStruct((ns, bsz, 1, D_LRU), F32),
        jax.ShapeDtypeStruct((ns, bsz, CONV_W - 1, D_LRU), F32),
    ]
    if apply_ln0:
        out_specs = [tile(D_MODEL)] + out_specs
        out_shape = [jax.ShapeDtypeStruct((bsz, seq, D_MODEL), F32)] + out_shape
    return pl.pallas_call(
        functools.partial(_prompt_pre_kernel, apply_ln0, tm),
        grid=(ns, bsz),
        in_specs=in_specs,
        out_specs=out_specs,
        out_shape=out_shape,
        scratch_shapes=[
            pltpu.VMEM((bsz, 1, D_LRU), F32),
            pltpu.VMEM((bsz, SUBLANES, D_LRU), F32),
            pltpu.VMEM((tm + SUBLANES, D_LRU), F32),
            pltpu.VMEM((tm, D_LRU), F32),
            pltpu.VMEM((tm, D_LRU), F32),
            pltpu.VMEM((tm, D_LRU), F32),
        ],
        compiler_params=pltpu.CompilerParams(
            dimension_semantics=("arbitrary", "arbitrary"), vmem_limit_bytes=VMEM_LIMIT),
        name="prompt_pre_ln0" if apply_ln0 else "prompt_pre",
    )(x, mod, ln0_g, ln0_b, lw["w_in"], lw["conv_w"], lw["conv_b"], lw["w_gates"], lw["b_a"],
      lw["b_x"], lw["lam"], lw["q_norm_g"], lw["w_uq"], lw["kv_norm_g"], lw["w_kv"],
      lw["g_lru"], cq_tab, sq_tab, ck_tab, sk_tab)


ATTN_HEADS_PER_STEP = 4


def _attn_kernel(tq, q_ref, k_ref, v_ref, o_ref):
    qi = pl.program_id(2)
    heads = range(ATTN_HEADS_PER_STEP)
    lanes = lambda hh: slice(hh * HEAD_PAD, (hh + 1) * HEAD_PAD)
    qs = [q_ref[0, :, lanes(hh)] for hh in heads]

    def tile_step(j, carry, masked):
        off = pl.multiple_of(j * tq, tq)
        scores, probs, out = {}, {}, {}

        def score(hh):
            kt = k_ref[0, pl.ds(off, tq), lanes(hh)]
            s = lax.dot_general(qs[hh], kt, (((1,), (1,)), ((), ())),
                                preferred_element_type=F32)
            if masked:
                row = lax.broadcasted_iota(jnp.int32, (tq, tq), 0)
                col = lax.broadcasted_iota(jnp.int32, (tq, tq), 1)
                s = jnp.where(col <= row, s, -jnp.inf)
            scores[hh] = s

        def softmax(hh):
            m_prev = carry[hh][0]
            m_new = jnp.maximum(m_prev, jnp.max(scores[hh], axis=1, keepdims=True))
            probs[hh] = (m_new, jnp.exp2(m_prev - m_new),
                         jnp.exp2(scores[hh] - m_new).astype(BF16))

        def weigh(hh):
            m_new, alpha, p = probs[hh]
            vt = v_ref[0, pl.ds(off, tq), lanes(hh)]
            out[hh] = (m_new, alpha * carry[hh][1] + _bdot(p, vt))

        n = len(heads)
        for t in range(n + 2):
            if t < n:
                score(t)
            if 0 <= t - 1 < n:
                softmax(t - 1)
            if 0 <= t - 2 < n:
                weigh(t - 2)
        return tuple(out[hh] for hh in heads)

    init = tuple((jnp.full((tq, 1), -jnp.inf, F32), jnp.zeros((tq, HEAD_PAD), F32))
                 for _ in heads)
    carry = lax.fori_loop(0, qi, lambda j, c: tile_step(j, c, False), init)
    final = tile_step(qi, carry, True)
    value_lane = lax.broadcasted_iota(jnp.int32, (tq, HEAD_PAD), 1) < D_V
    for hh in heads:
        acc = final[hh][1]
        o_ref[0, :, lanes(hh)] = jnp.where(value_lane, acc / acc[:, D_V:D_V + 1], 0.0)


def _prompt_attention(q, k, v):
    bsz, seq, _ = q.shape
    tq = _prompt_tile(seq)
    width = ATTN_HEADS_PER_STEP * HEAD_PAD
    return pl.pallas_call(
        functools.partial(_attn_kernel, tq),
        grid=(bsz, N_HEADS // ATTN_HEADS_PER_STEP, seq // tq),
        in_specs=[
            pl.BlockSpec((1, tq, width), lambda b, h, i: (b, i, h)),
            pl.BlockSpec((1, seq, width), lambda b, h, i: (b, 0, h)),
            pl.BlockSpec((1, seq, width), lambda b, h, i: (b, 0, h)),
        ],
        out_specs=pl.BlockSpec((1, tq, width), lambda b, h, i: (b, i, h)),
        out_shape=jax.ShapeDtypeStruct((bsz, seq, D_HEADS_PAD), F32),
        compiler_params=pltpu.CompilerParams(
            dimension_semantics=("arbitrary", "arbitrary", "arbitrary"),
            vmem_limit_bytes=VMEM_LIMIT),
        name="prompt_attention",
    )(q, k, v)


def _post_kernel(from_latent, ff_chunk, x_ref, mod_ref, mixl_ref, att_ref, wuv_ref, gatt_ref,
                 wol_ref, woa_ref, lng_ref, lnb_ref, wup_ref, wdn_ref, o_ref):
    x = x_ref[0]
    if from_latent:
        ya = jnp.concatenate(
            [_bdot(att_ref[hh], wuv_ref[hh]) for hh in range(N_HEADS)], axis=1)
    else:
        ya = att_ref[0]
    na = _rms_norm(ya, gatt_ref[...], N_HEADS * D_V).astype(BF16)
    o = _bdot(mixl_ref[0], wol_ref[...]) + _bdot(na, woa_ref[...])
    x1 = _layer_norm(ALPHA * x + mod_ref[2, 0] * o, lng_ref[0:1, :], lnb_ref[0:1, :])
    h2 = (x1 * (1.0 + mod_ref[4, 0]) + mod_ref[3, 0]).astype(BF16)
    f = jnp.zeros(x.shape, F32)
    for c in range(D_FF // ff_chunk):
        up = _bdot(h2, wup_ref[:, c * ff_chunk:(c + 1) * ff_chunk])
        act = jnp.square(jnp.maximum(up, 0.0)).astype(BF16)
        f = f + _bdot(act, wdn_ref[c * ff_chunk:(c + 1) * ff_chunk, :])
    o_ref[0] = _layer_norm(ALPHA * x1 + mod_ref[5, 0] * f, lng_ref[1:2, :], lnb_ref[1:2, :])


def _post(from_latent, x, mod, mixl, att, lw):
    groups, rows, _ = x.shape
    tm = _prompt_tile(rows)
    tile = lambda w: pl.BlockSpec((1, tm, w), lambda g, s: (g, s, 0))
    mod_rows = mod.shape[2]
    if from_latent:
        att_spec = pl.BlockSpec((N_HEADS, tm, KV_RANK), lambda g, s: (0, s, 0))
    else:
        att_spec = tile(D_HEADS_PAD)
    return pl.pallas_call(
        functools.partial(_post_kernel, from_latent, 1024),
        grid=(groups, rows // tm),
        in_specs=[
            tile(D_MODEL),
            pl.BlockSpec((6, 1, mod_rows, D_MODEL), lambda g, s: (0, g, 0, 0)),
            tile(D_LRU), att_spec,
            _const_spec(lw["w_uv_heads"].shape), _const_spec((1, D_HEADS_PAD)),
            _const_spec(lw["w_o_lru"].shape), _const_spec(lw["w_o_att"].shape),
            _const_spec((2, D_MODEL)), _const_spec((2, D_MODEL)),
            _const_spec(lw["w_up"].shape), _const_spec(lw["w_down"].shape),
        ],
        out_specs=tile(D_MODEL),
        out_shape=jax.ShapeDtypeStruct(x.shape, F32),
        compiler_params=pltpu.CompilerParams(
            dimension_semantics=("arbitrary", "arbitrary"), vmem_limit_bytes=VMEM_LIMIT),
        name="sample_post" if from_latent else "prompt_post",
    )(x, mod, mixl, att, lw["w_uv_heads"], lw["g_att"], lw["w_o_lru"], lw["w_o_att"],
      lw["ln_g"], lw["ln_b"], lw["w_up"], lw["w_down"])


def _sample_pre_kernel(apply_ln0, *refs):
    (x_ref, mod_ref, ln0g_ref, ln0b_ref, win_ref, convw_ref, convb_ref, wg_ref, ba_ref,
     bx_ref, lam_ref, qg_ref, wuq_ref, kvg_ref, wukt_ref, glru_ref,
     cq_ref, sq_ref, ck_ref, sk_ref, h0_ref, buf_ref) = refs[:22]
    outs = refs[22:]
    if apply_ln0:
        xn_ref, outs = outs[0], outs[1:]
    q_ref, qlat_ref, mixl_ref, ckv_ref, kr_ref, hnew_ref, conv_ref = outs

    x = x_ref[...]
    if apply_ln0:
        x = _layer_norm(x, ln0g_ref[...], ln0b_ref[...])
        xn_ref[...] = x
    h = (x * (1.0 + mod_ref[1, 0]) + mod_ref[0, 0]).astype(BF16)
    proj = _bdot(h, win_ref[...])
    u_raw = proj[:, OFF_U:OFF_GATE]
    gate = proj[:, OFF_GATE:OFF_CQ]

    uc = convb_ref[...] + u_raw * convw_ref[CONV_W - 1:CONV_W, :]
    for kk in range(CONV_W - 1):
        uc = uc + buf_ref[kk] * convw_ref[kk:kk + 1, :]
    for kk in range(CONV_W - 2):
        conv_ref[kk] = buf_ref[kk + 1]
    conv_ref[CONV_W - 2] = u_raw

    a, bb = _lru_coeffs(uc, wg_ref, ba_ref, bx_ref, lam_ref)
    h_new = a * h0_ref[...] + bb
    hnew_ref[...] = h_new
    y = h_new * _gelu_tanh(gate)
    mixl_ref[...] = _rms_norm(y, glru_ref[...], D_LRU).astype(BF16)

    q, ckvn, krw = _latent_heads(proj, cq_ref[0:1, :], sq_ref[0:1, :], ck_ref[0:1, :],
                                 sk_ref[0:1, :], qg_ref, wuq_ref, kvg_ref)
    qb = q.astype(BF16)
    q_ref[...] = qb
    ckv_ref[...] = ckvn
    kr_ref[...] = krw[:, OFF_KR_LANE:OFF_KR_LANE + D_ROPE]
    for hh in range(N_HEADS):
        qlat_ref[hh] = _bdot(qb[:, hh * HEAD_PAD:(hh + 1) * HEAD_PAD], wukt_ref[hh])


def _sample_pre(apply_ln0, x, mod, ln0_g, ln0_b, lw, tabs, h0, buf):
    rows = x.shape[0]
    cq_tab, sq_tab, ck_tab, sk_tab = tabs
    full = lambda a: _const_spec(a.shape)
    args = (x, mod, ln0_g, ln0_b, lw["w_in"], lw["conv_w"], lw["conv_b"], lw["w_gates"],
            lw["b_a"], lw["b_x"], lw["lam"], lw["q_norm_g"], lw["w_uq"], lw["kv_norm_g"],
            lw["w_uk_t"], lw["g_lru"], cq_tab, sq_tab, ck_tab, sk_tab, h0, buf)
    out_shape = [
        jax.ShapeDtypeStruct((rows, D_HEADS_PAD), BF16),
        jax.ShapeDtypeStruct((N_HEADS, rows, KV_RANK), F32),
        jax.ShapeDtypeStruct((rows, D_LRU), BF16),
        jax.ShapeDtypeStruct((rows, KV_RANK), F32),
        jax.ShapeDtypeStruct((rows, D_ROPE), F32),
        jax.ShapeDtypeStruct((rows, D_LRU), F32),
        jax.ShapeDtypeStruct((CONV_W - 1, rows, D_LRU), F32),
    ]
    if apply_ln0:
        out_shape = [jax.ShapeDtypeStruct((rows, D_MODEL), F32)] + out_shape
    return pl.pallas_call(
        functools.partial(_sample_pre_kernel, apply_ln0),
        grid=(1,),
        in_specs=[full(a) for a in args],
        out_specs=[_const_spec(s.shape) for s in out_shape],
        out_shape=out_shape,
        compiler_params=pltpu.CompilerParams(
            dimension_semantics=("arbitrary",), vmem_limit_bytes=VMEM_LIMIT),
        name="sample_pre_ln0" if apply_ln0 else "sample_pre",
    )(*args)


PAGED_SLOTS = 4


def _paged_attn_kernel(layer, pages_per_chunk, chunks_per_seq, page_size,
                       pt_ref, qlat_ref, qrope_ref, cnew_ref, knew_ref, cache_c, cache_kt,
                       o_ref, *scratch):
    cbufs = scratch[:PAGED_SLOTS]
    kbufs = scratch[PAGED_SLOTS:2 * PAGED_SLOTS]
    sem_c, sem_k, m_sc, l_sc, acc_sc = scratch[2 * PAGED_SLOTS:]
    g = pl.program_id(0)
    n_steps = pl.num_programs(0)
    total_chunks = PAGED_SLOTS * n_steps
    bufs = tuple(zip(cbufs, kbufs))
    ahead = PAGED_SLOTS - 1

    def page_copies(chunk_idx, sl, j):
        seq = chunk_idx // chunks_per_seq
        page = pt_ref[seq, (chunk_idx % chunks_per_seq) * pages_per_chunk + j]
        span = pl.ds(j * page_size, page_size)
        cb, kb = bufs[sl]
        return (pltpu.make_async_copy(cache_c.at[layer, page], cb.at[span, :], sem_c.at[sl]),
                pltpu.make_async_copy(cache_kt.at[layer, page], kb.at[:, span], sem_k.at[sl]))

    def start_chunk(chunk_idx, sl):
        for j in range(pages_per_chunk):
            for cp in page_copies(chunk_idx, sl, j):
                cp.start()

    def wait_chunk(chunk_idx, sl):
        for j in range(pages_per_chunk):
            for cp in page_copies(chunk_idx, sl, j):
                cp.wait()

    qlat = qlat_ref[0]
    qrope = qrope_ref[0]

    def consume(sl, m_prev, l_prev, acc_prev):
        cb, kb = bufs[sl]
        ck = cb[...].astype(BF16)
        s = (lax.dot_general(qlat, ck, (((1,), (1,)), ((), ())), preferred_element_type=F32)
             + _bdot(qrope, kb[...].astype(BF16)))
        m_new = jnp.maximum(m_prev, jnp.max(s, axis=1, keepdims=True))
        alpha = jnp.exp2(m_prev - m_new)
        p = jnp.exp2(s - m_new)
        l_new = alpha * l_prev + jnp.sum(p, axis=1, keepdims=True)
        acc = alpha * acc_prev + _bdot(p.astype(BF16), ck)
        return m_new, l_new, acc

    first = PAGED_SLOTS * g

    @pl.when(g == 0)
    def _():
        for k in range(ahead):
            start_chunk(k, k)

    cnew = cnew_ref[0].astype(BF16).astype(F32)
    knew = knew_ref[0].astype(BF16).astype(F32)
    s_new = (jnp.sum(qlat.astype(F32) * cnew, axis=1, keepdims=True)
             + jnp.sum(qrope.astype(F32) * knew, axis=1, keepdims=True))
    opens = lax.rem(first, chunks_per_seq) == 0
    state = (jnp.where(opens, s_new, m_sc[...]),
             jnp.where(opens, jnp.ones_like(s_new), l_sc[...]),
             jnp.where(opens, jnp.broadcast_to(cnew, acc_sc.shape), acc_sc[...]))

    for k in range(PAGED_SLOTS):
        wait_chunk(first + k, k)
        state = consume(k, *state)
        start_chunk(lax.rem(first + k + ahead, total_chunks), (k + ahead) % PAGED_SLOTS)

    m_fin, l_fin, acc_fin = state
    m_sc[...] = m_fin
    l_sc[...] = l_fin
    acc_sc[...] = acc_fin
    o_ref[0] = acc_fin / l_fin

    @pl.when(g == n_steps - 1)
    def _():
        for k in range(ahead):
            wait_chunk(k, k)


def _paged_pages_per_chunk(n_pages):
    for p in (32, 16, 8, 4, 2, 1):
        if n_pages % (PAGED_SLOTS * p) == 0:
            return p
    raise ValueError(f"the page count per sequence ({n_pages}) must be a multiple of {PAGED_SLOTS}")


def _paged_attention(layer, page_table, qlat, qrope, cnew, knew, cache_ckv, cache_krope_t):
    nseq, n_pages = page_table.shape
    page_size = cache_ckv.shape[2]
    ppc = _paged_pages_per_chunk(n_pages)
    chunks_per_seq = n_pages // ppc
    steps_per_seq = chunks_per_seq // PAGED_SLOTS
    per_seq = lambda w: pl.BlockSpec((1, N_HEADS, w), lambda s, pt: (s // steps_per_seq, 0, 0))
    one_row = lambda w: pl.BlockSpec((1, 1, w), lambda s, pt: (s // steps_per_seq, 0, 0))
    grid_spec = pltpu.PrefetchScalarGridSpec(
        num_scalar_prefetch=1,
        grid=(nseq * steps_per_seq,),
        in_specs=[
            per_seq(KV_RANK), per_seq(D_ROPE), one_row(KV_RANK), one_row(D_ROPE),
            pl.BlockSpec(memory_space=pl.ANY), pl.BlockSpec(memory_space=pl.ANY),
        ],
        out_specs=per_seq(KV_RANK),
        scratch_shapes=(
            [pltpu.VMEM((ppc * page_size, KV_RANK), F32)] * PAGED_SLOTS
            + [pltpu.VMEM((D_ROPE, ppc * page_size), F32)] * PAGED_SLOTS
        ) + [
            pltpu.SemaphoreType.DMA((PAGED_SLOTS,)),
            pltpu.SemaphoreType.DMA((PAGED_SLOTS,)),
            pltpu.VMEM((N_HEADS, 1), F32),
            pltpu.VMEM((N_HEADS, 1), F32),
            pltpu.VMEM((N_HEADS, KV_RANK), F32),
        ],
    )
    return pl.pallas_call(
        functools.partial(_paged_attn_kernel, layer, ppc, chunks_per_seq, page_size),
        grid_spec=grid_spec,
        out_shape=jax.ShapeDtypeStruct((nseq, N_HEADS, KV_RANK), F32),
        compiler_params=pltpu.CompilerParams(
            dimension_semantics=("arbitrary",), vmem_limit_bytes=VMEM_LIMIT),
        name="paged_attention",
    )(page_table, qlat, qrope, cnew, knew, cache_ckv, cache_krope_t)


def _pad_heads(w, width):
    pad = [(0, 0)] * (w.ndim - 1) + [(0, HEAD_PAD - width)]
    return jnp.pad(w, pad).reshape(*w.shape[:-2], D_HEADS_PAD)


def _block_diag_chunks(w):
    per = GATE_CHUNK // LRU_BLOCK
    eye = jnp.eye(per, dtype=w.dtype)
    w = w.reshape(D_LRU // GATE_CHUNK, per, LRU_BLOCK, LRU_BLOCK)
    return jnp.einsum("cnij,nm->cnimj", w, eye).reshape(-1, GATE_CHUNK, GATE_CHUNK)


def _layer_weights(l, p):
    w_in = p["w_in"][l]
    kr_cols = w_in[:, OFF_KRA:OFF_KRA + D_ROPE]
    kr_swap = jnp.concatenate([-kr_cols[:, HALF_ROPE:], kr_cols[:, :HALF_ROPE]], axis=1)
    lane_pad = lambda w: jnp.pad(w, ((0, 0), (OFF_KR_LANE, LANES - OFF_KR_LANE - w.shape[1])))
    w_in_pad = jnp.concatenate([w_in[:, :OFF_KRA], lane_pad(kr_cols), lane_pad(kr_swap)], axis=1)

    w_uq = p["w_uq"][l]
    x1 = w_uq[..., D_NOPE:D_NOPE + HALF_ROPE]
    x2 = w_uq[..., D_NOPE + HALF_ROPE:]
    w_uq_swap = jnp.concatenate([jnp.zeros_like(w_uq[..., :D_NOPE]), -x2, x1], axis=-1)
    w_uq_pad = jnp.concatenate([_pad_heads(w_uq, D_QK), _pad_heads(w_uq_swap, D_QK)], axis=1)

    w_uk = p["w_uk"][l]
    w_uv = p["w_uv"][l]
    w_kv = jnp.concatenate([_pad_heads(w_uk, D_NOPE), _pad_heads(w_uv, D_V)], axis=1)

    w_uk_t = jnp.pad(jnp.transpose(w_uk, (1, 2, 0)), ((0, 0), (0, HEAD_PAD - D_NOPE), (0, 0)))
    w_uv_heads = jnp.pad(jnp.transpose(w_uv, (1, 0, 2)), ((0, 0), (0, 0), (0, HEAD_PAD - D_V)))

    w_o = p["w_o"][l]
    w_o_att = jnp.pad(w_o[D_LRU:].reshape(N_HEADS, D_V, D_MODEL),
                      ((0, 0), (0, HEAD_PAD - D_V), (0, 0))).reshape(D_HEADS_PAD, D_MODEL)
    g_att = _pad_heads(p["g_att"][l].reshape(1, N_HEADS, D_V), D_V)
    row = lambda v: v.reshape(1, -1)
    return {
        "w_in": w_in_pad.astype(BF16),
        "conv_w": p["conv_w"][l], "conv_b": row(p["conv_b"][l]),
        "w_gates": jnp.stack([_block_diag_chunks(p["w_a"][l]),
                              _block_diag_chunks(p["w_x"][l])]).astype(BF16),
        "b_a": row(p["b_a"][l]), "b_x": row(p["b_x"][l]), "lam": row(p["lru_lambda"][l]),
        "q_norm_g": row(p["q_norm_g"][l]), "w_uq": w_uq_pad.astype(BF16),
        "kv_norm_g": row(p["kv_norm_g"][l]), "w_kv": w_kv.astype(BF16),
        "w_uk_t": w_uk_t.astype(BF16), "w_uv_heads": w_uv_heads.astype(BF16),
        "g_lru": row(p["g_lru"][l]), "g_att": g_att,
        "w_o_lru": w_o[:D_LRU].astype(BF16), "w_o_att": w_o_att.astype(BF16),
        "ln_g": p["ln_g"][l], "ln_b": p["ln_b"][l],
        "w_up": p["w_up"][l].astype(BF16), "w_down": p["w_down"][l].astype(BF16),
    }


def kernel(x_prompt, x_sample, cache_ckv, cache_krope, state_lru_h, state_conv, page_table,
           c_prompt, c_sample, ln0_g, ln0_b, w_ada, b_ada, w_in, conv_w, conv_b, w_a, b_a,
           w_x, b_x, lru_lambda, q_norm_g, w_uq, kv_norm_g, w_uk, w_uv, g_lru, g_att, w_o,
           ln_g, ln_b, w_up, w_down):
    params = dict(w_in=w_in, conv_w=conv_w, conv_b=conv_b, w_a=w_a, b_a=b_a, w_x=w_x, b_x=b_x,
                  lru_lambda=lru_lambda, q_norm_g=q_norm_g, w_uq=w_uq, kv_norm_g=kv_norm_g,
                  w_uk=w_uk, w_uv=w_uv, g_lru=g_lru, g_att=g_att, w_o=w_o, ln_g=ln_g,
                  ln_b=ln_b, w_up=w_up, w_down=w_down)
    b_p, s_p, _ = x_prompt.shape
    b_s, s_s, _ = x_sample.shape
    if s_s != 1:
        raise ValueError("the sample group is a single-token decode step")
    depth = w_ada.shape[0]
    past_len = page_table.shape[1] * cache_ckv.shape[2]

    mods = _ada_mod(jnp.concatenate([c_prompt, c_sample], axis=0), w_ada, b_ada)
    tabs_p = _rope_tables(s_p, 0)
    tabs_s = _rope_tables(SUBLANES, past_len)
    ln0g, ln0b = ln0_g.reshape(1, -1), ln0_b.reshape(1, -1)
    cache_krope_t = jnp.swapaxes(cache_krope, 2, 3)

    xp = x_prompt
    xs = x_sample.reshape(b_s, D_MODEL)
    outs_p = [[] for _ in range(4)]
    outs_s = [[] for _ in range(4)]
    for l in range(depth):
        lw = _layer_weights(l, params)
        first = l == 0
        mod_p = mods[l, :, :b_p].reshape(6, b_p, 1, D_MODEL)
        mod_s = mods[l, :, b_p:].reshape(6, 1, b_s, D_MODEL)

        res = _prompt_pre(first, xp, mod_p, ln0g, ln0b, lw, tabs_p)
        if first:
            xp, res = res[0], res[1:]
        q, k, v, mixl, ckv, kr, h_last, conv = res
        att = _prompt_attention(q, k, v)
        xp = _post(False, xp, mod_p, mixl, att, lw)
        for dst, val in zip(outs_p, (ckv, kr, h_last[-1].reshape(b_p, D_LRU), conv[-1])):
            dst.append(val)

        res = _sample_pre(first, xs, mod_s, ln0g, ln0b, lw, tabs_s, state_lru_h[l],
                          jnp.transpose(state_conv[l], (1, 0, 2)))
        if first:
            xs, res = res[0], res[1:]
        q_s, qlat, mixl_s, ckv_s, kr_s, h_new, conv_s = res
        qlat_b = jnp.transpose(qlat, (1, 0, 2)).astype(BF16)
        qrope_b = q_s.reshape(b_s, N_HEADS, HEAD_PAD)[:, :, D_NOPE:D_QK]
        o_lat = _paged_attention(l, page_table, qlat_b, qrope_b, ckv_s.reshape(b_s, 1, KV_RANK),
                                 kr_s.reshape(b_s, 1, D_ROPE), cache_ckv, cache_krope_t)
        o_lat_h = jnp.transpose(o_lat, (1, 0, 2)).astype(BF16)
        xs = _post(True, xs.reshape(1, b_s, D_MODEL), mod_s, mixl_s.reshape(1, b_s, D_LRU),
                   o_lat_h, lw).reshape(b_s, D_MODEL)
        for dst, val in zip(outs_s, (ckv_s.reshape(b_s, 1, KV_RANK), kr_s.reshape(b_s, 1, D_ROPE),
                                     h_new, jnp.transpose(conv_s, (1, 0, 2)))):
            dst.append(val)

    stack = lambda vals: jnp.stack(vals, axis=0)
    return (xp, xs.reshape(b_s, 1, D_MODEL),
            stack(outs_p[0]), stack(outs_p[1]), stack(outs_p[2]), stack(outs_p[3]),
            stack(outs_s[0]), stack(outs_s[1]), stack(outs_s[2]), stack(outs_s[3]))
```

```python
import functools
import math

import jax
import jax.numpy as jnp
import numpy as np
from jax import lax
from jax.experimental import pallas as pl
from jax.experimental.pallas import tpu as pltpu

D_MODEL = 1024
D_LRU = 512
LRU_BLOCKS = 8
LRU_BLOCK = D_LRU // LRU_BLOCKS
CONV_W = 4
LRU_C = 8.0
N_HEADS = 8
D_NOPE = 64
D_ROPE = 32
D_QK = D_NOPE + D_ROPE
D_V = 64
Q_RANK = 384
KV_RANK = 256
ROPE_THETA = 10000.0
D_FF = 4 * D_MODEL
DEPTH = 2
ALPHA = (2 * DEPTH) ** 0.25
ATT_SCALE = D_QK ** -0.5
EPS = 1e-6

LANES = 128
SUBLANES = 8
VMEM_BYTES_V7X = 64 * 1024 * 1024
VMEM_LIMIT = VMEM_BYTES_V7X - 8 * 1024 * 1024

HEAD_PAD = LANES
D_HEADS_PAD = N_HEADS * HEAD_PAD
HALF_ROPE = D_ROPE // 2
OFF_U, OFF_GATE, OFF_CQ, OFF_CKV = 0, D_LRU, 2 * D_LRU, 2 * D_LRU + Q_RANK
OFF_KRA = OFF_CKV + KV_RANK
OFF_KRB = OFF_KRA + LANES
D_IN_PAD = OFF_KRB + LANES
EXP2_SCALE = ATT_SCALE * math.log2(math.e)
GATE_CHUNK = 256
OFF_KR_LANE = D_NOPE

F32 = jnp.float32
BF16 = jnp.bfloat16


def _prompt_tile(seq):
    for t in (512, 256, 128, 64, 32, 16, 8):
        if seq % t == 0:
            return t
    raise ValueError(f"sequence length {seq} must be a multiple of 8")


def _const_spec(shape):
    nd = len(shape)
    return pl.BlockSpec(shape, lambda *_: (0,) * nd, pipeline_mode=pl.Buffered(1))


def _layer_norm(x, g, b):
    mu = jnp.mean(x, axis=-1, keepdims=True)
    xc = x - mu
    var = jnp.mean(xc * xc, axis=-1, keepdims=True)
    return xc * lax.rsqrt(var + EPS) * g + b


def _rms_norm(x, g, n):
    ms = jnp.sum(x * x, axis=-1, keepdims=True) * (1.0 / n)
    return x * lax.rsqrt(ms + EPS) * g


def _sigmoid(x):
    return 1.0 / (1.0 + jnp.exp(-x))


def _softplus(x):
    return jnp.maximum(x, 0.0) + jnp.log1p(jnp.exp(-jnp.abs(x)))


def _gelu_tanh(x):
    c = math.sqrt(2.0 / math.pi)
    return 0.5 * x * (1.0 + jnp.tanh(c * (x + 0.044715 * (x * x * x))))


def _bdot(a, b):
    return jnp.dot(a, b, preferred_element_type=F32)


def _lru_coeffs(uc, wg_ref, ba_ref, bx_ref, lam_ref):
    ub = uc.astype(BF16)
    chunks = [ub[:, c * GATE_CHUNK:(c + 1) * GATE_CHUNK] for c in range(D_LRU // GATE_CHUNK)]
    gate = lambda n: jnp.concatenate(
        [_bdot(x, wg_ref[n, c]) for c, x in enumerate(chunks)], axis=1)
    r = _sigmoid(gate(0) + ba_ref[...])
    i = _sigmoid(gate(1) + bx_ref[...])
    log_a = (-LRU_C) * r * _softplus(-lam_ref[...])
    a = jnp.exp(log_a)
    t = jnp.tanh(log_a)
    b = jnp.sqrt((-2.0 * t) / (1.0 - t)) * (i * uc)
    return a, b


def _latent_heads(proj, cq_tab, sq_tab, ck_tab, sk_tab, qg_ref, wuq_ref, kvg_ref):
    cut = lambda lo, hi: proj[:, lo - OFF_CQ:hi - OFF_CQ]
    cqn = _rms_norm(cut(OFF_CQ, OFF_CKV), qg_ref[...], Q_RANK)
    qq = _bdot(cqn.astype(BF16), wuq_ref[...])
    q = qq[:, :D_HEADS_PAD] * cq_tab + qq[:, D_HEADS_PAD:] * sq_tab
    ckvn = _rms_norm(cut(OFF_CKV, OFF_KRA), kvg_ref[...], KV_RANK)
    krw = cut(OFF_KRA, OFF_KRB) * ck_tab + cut(OFF_KRB, D_IN_PAD) * sk_tab
    return q, ckvn, krw


def _ada_kernel(c_ref, w_ref, b_ref, o_ref):
    c = c_ref[...]
    s = (c * _sigmoid(c)).astype(BF16)
    o_ref[0, 0] = _bdot(s, w_ref[0].astype(BF16)) + b_ref[0]


def _ada_mod(c_all, w_ada, b_ada):
    rows = c_all.shape[0]
    depth, _, n_out = w_ada.shape
    n_mod = n_out // D_MODEL
    return pl.pallas_call(
        _ada_kernel,
        grid=(depth, n_mod),
        in_specs=[
            pl.BlockSpec((rows, D_MODEL), lambda l, j: (0, 0)),
            pl.BlockSpec((1, D_MODEL, D_MODEL), lambda l, j: (l, 0, j)),
            pl.BlockSpec((1, 1, D_MODEL), lambda l, j: (l, 0, j)),
        ],
        out_specs=pl.BlockSpec((1, 1, rows, D_MODEL), lambda l, j: (l, j, 0, 0)),
        out_shape=jax.ShapeDtypeStruct((depth, n_mod, rows, D_MODEL), F32),
        compiler_params=pltpu.CompilerParams(
            dimension_semantics=("arbitrary", "arbitrary"), vmem_limit_bytes=VMEM_LIMIT),
        name="ada_mod",
    )(c_all, w_ada, b_ada.reshape(depth, 1, n_out))


def _rope_table_kernel(pos0, inv_ref, one_ref, rope_ref, cq_ref, sq_ref, ck_ref, sk_ref):
    rows = ck_ref.shape[0]
    pos = (lax.broadcasted_iota(jnp.int32, (rows, 1), 0)
           + (pl.program_id(0) * rows + pos0)).astype(F32)
    ang = pos * inv_ref[...]
    cos = rope_ref[...] * jnp.cos(ang)
    sin = rope_ref[...] * jnp.sin(ang)
    ck_ref[...] = cos
    sk_ref[...] = sin
    cq_ref[...] = jnp.concatenate([(cos + one_ref[...]) * EXP2_SCALE] * N_HEADS, axis=1)
    sq_ref[...] = jnp.concatenate([sin * EXP2_SCALE] * N_HEADS, axis=1)


def _rope_tables(rows, pos0):
    inv = ROPE_THETA ** (-jnp.arange(HALF_ROPE, dtype=F32) / HALF_ROPE)
    zeros = lambda n: jnp.zeros((n,), F32)
    ones = lambda n: jnp.ones((n,), F32)
    head_inv = jnp.concatenate([zeros(D_NOPE), inv, inv, zeros(HEAD_PAD - D_QK)])[None, :]
    head_rope = jnp.concatenate([zeros(D_NOPE), ones(D_ROPE), zeros(HEAD_PAD - D_QK)])[None, :]
    head_one = jnp.concatenate([ones(D_NOPE), zeros(HEAD_PAD - D_NOPE)])[None, :]
    tr = _prompt_tile(rows)
    widths = (D_HEADS_PAD, D_HEADS_PAD, LANES, LANES)
    return pl.pallas_call(
        functools.partial(_rope_table_kernel, pos0),
        grid=(rows // tr,),
        in_specs=[_const_spec((1, LANES))] * 3,
        out_specs=[pl.BlockSpec((tr, w), lambda i: (i, 0)) for w in widths],
        out_shape=[jax.ShapeDtypeStruct((rows, w), F32) for w in widths],
        compiler_params=pltpu.CompilerParams(dimension_semantics=("arbitrary",)),
        name="rope_tables",
    )(head_inv, head_one, head_rope)


def _prompt_pre_kernel(apply_ln0, tm, *refs):
    (x_ref, mod_ref, ln0g_ref, ln0b_ref, win_ref, convw_ref, convb_ref, wg_ref, ba_ref,
     bx_ref, lam_ref, qg_ref, wuq_ref, kvg_ref, wkv_ref, glru_ref,
     cq_ref, sq_ref, ck_ref, sk_ref) = refs[:20]
    outs = refs[20:]
    if apply_ln0:
        xn_ref, outs = outs[0], outs[1:]
    (q_ref, k_ref, v_ref, mixl_ref, ckv_ref, kr_ref, hlast_ref, conv_ref,
     hcar_ref, halo_ref, ubuf_ref, a_sc, b_sc, hs_sc) = outs

    s_idx = pl.program_id(0)
    b_idx = pl.program_id(1)

    @pl.when(s_idx == 0)
    def _():
        halo_ref[b_idx] = jnp.zeros((SUBLANES, D_LRU), F32)
        hcar_ref[b_idx] = jnp.zeros((1, D_LRU), F32)

    x = x_ref[0]
    if apply_ln0:
        x = _layer_norm(x, ln0g_ref[...], ln0b_ref[...])
        xn_ref[0] = x
    h = (x * (1.0 + mod_ref[1, 0]) + mod_ref[0, 0]).astype(BF16)
    proj = _bdot(h, win_ref[...])
    u_raw = proj[:, OFF_U:OFF_GATE]
    gate = proj[:, OFF_GATE:OFF_CQ]

    q, ckvn, krw = _latent_heads(proj[:, OFF_CQ:], cq_ref[...], sq_ref[...], ck_ref[...],
                                 sk_ref[...], qg_ref, wuq_ref, kvg_ref)
    q_ref[0] = q.astype(BF16)
    ckv_ref[0] = ckvn
    kr_ref[0] = krw[:, OFF_KR_LANE:OFF_KR_LANE + D_ROPE]
    kv = _bdot(ckvn.astype(BF16), wkv_ref[...])
    k_ref[0] = (kv[:, :D_HEADS_PAD] + jnp.concatenate([krw] * N_HEADS, axis=1)).astype(BF16)
    lane = lax.broadcasted_iota(jnp.int32, (1, D_HEADS_PAD), 1) & (HEAD_PAD - 1)
    v_ref[0] = (kv[:, D_HEADS_PAD:] + (lane == D_V).astype(F32)).astype(BF16)

    ubuf_ref[0:SUBLANES, :] = halo_ref[b_idx]
    ubuf_ref[SUBLANES:, :] = u_raw
    halo_ref[b_idx] = u_raw[tm - SUBLANES:, :]
    uc = convb_ref[...] + u_raw * convw_ref[CONV_W - 1:CONV_W, :]
    for kk in range(CONV_W - 1):
        off = SUBLANES - (CONV_W - 1) + kk
        uc = uc + ubuf_ref[off:off + tm, :] * convw_ref[kk:kk + 1, :]
    conv_ref[0, 0] = ubuf_ref[tm + SUBLANES - (CONV_W - 1):tm + SUBLANES, :]

    a, bb = _lru_coeffs(uc, wg_ref, ba_ref, bx_ref, lam_ref)

    groups = (tm // SUBLANES, SUBLANES, D_LRU)
    a = a.reshape(groups)
    bb = bb.reshape(groups)
    row8 = lax.broadcasted_iota(jnp.int32, (1, SUBLANES, D_LRU), 1)
    for d in (1, 2, 4):
        a_sh = pltpu.roll(a, d, axis=1)
        b_sh = pltpu.roll(bb, d, axis=1)
        take = row8 >= d
        bb = jnp.where(take, a * b_sh + bb, bb)
        a = jnp.where(take, a * a_sh, a)
    a_sc[...] = a.reshape(tm, D_LRU)
    b_sc[...] = bb.reshape(tm, D_LRU)
    gate_act = _gelu_tanh(gate)

    def group_step(g, hprev):
        off = pl.multiple_of(g * SUBLANES, SUBLANES)
        hg = a_sc[pl.ds(off, SUBLANES), :] * hprev + b_sc[pl.ds(off, SUBLANES), :]
        hs_sc[pl.ds(off, SUBLANES), :] = hg
        return hg[SUBLANES - 1:SUBLANES, :]

    h_fin = lax.fori_loop(0, tm // SUBLANES, group_step, hcar_ref[b_idx], unroll=8)
    hcar_ref[b_idx] = h_fin
    hlast_ref[0, 0] = h_fin

    y = hs_sc[...] * gate_act
    mixl_ref[0] = _rms_norm(y, glru_ref[...], D_LRU).astype(BF16)


def _prompt_pre(apply_ln0, x, mod, ln0_g, ln0_b, lw, tabs):
    bsz, seq, _ = x.shape
    tm = _prompt_tile(seq)
    ns = seq // tm
    cq_tab, sq_tab, ck_tab, sk_tab = tabs
    tile = lambda w: pl.BlockSpec((1, tm, w), lambda s, b: (b, s, 0))
    in_specs = [
        tile(D_MODEL),
        pl.BlockSpec((6, 1, 1, D_MODEL), lambda s, b: (0, b, 0, 0)),
        _const_spec((1, D_MODEL)), _const_spec((1, D_MODEL)),
        _const_spec(lw["w_in"].shape), _const_spec((CONV_W, D_LRU)), _const_spec((1, D_LRU)),
        _const_spec(lw["w_gates"].shape), _const_spec((1, D_LRU)), _const_spec((1, D_LRU)),
        _const_spec((1, D_LRU)), _const_spec((1, Q_RANK)), _const_spec(lw["w_uq"].shape),
        _const_spec((1, KV_RANK)), _const_spec(lw["w_kv"].shape), _const_spec((1, D_LRU)),
        pl.BlockSpec((tm, D_HEADS_PAD), lambda s, b: (s, 0)),
        pl.BlockSpec((tm, D_HEADS_PAD), lambda s, b: (s, 0)),
        pl.BlockSpec((tm, LANES), lambda s, b: (s, 0)),
        pl.BlockSpec((tm, LANES), lambda s, b: (s, 0)),
    ]
    out_specs = [
        tile(D_HEADS_PAD), tile(D_HEADS_PAD), tile(D_HEADS_PAD), tile(D_LRU), tile(KV_RANK),
        tile(D_ROPE),
        pl.BlockSpec((1, 1, 1, D_LRU), lambda s, b: (s, b, 0, 0)),
        pl.BlockSpec((1, 1, CONV_W - 1, D_LRU), lambda s, b: (s, b, 0, 0)),
    ]
    out_shape = [
        jax.ShapeDtypeStruct((bsz, seq, D_HEADS_PAD), BF16),
        jax.ShapeDtypeStruct((bsz, seq, D_HEADS_PAD), BF16),
        jax.ShapeDtypeStruct((bsz, seq, D_HEADS_PAD), BF16),
        jax.ShapeDtypeStruct((bsz, seq, D_LRU), BF16),
        jax.ShapeDtypeStruct((bsz, seq, KV_RANK), F32),
        jax.ShapeDtypeStruct((bsz, seq, D_ROPE), F32),
        jax.ShapeDtypeStruct((ns, bsz, 1, D_LRU), F32),
        jax.ShapeDtypeStruct((ns, bsz, CONV_W - 1, D_LRU), F32),
    ]
    if apply_ln0:
        out_specs = [tile(D_MODEL)] + out_specs
        out_shape = [jax.ShapeDtypeStruct((bsz, seq, D_MODEL), F32)] + out_shape
    return pl.pallas_call(
        functools.partial(_prompt_pre_kernel, apply_ln0, tm),
        grid=(ns, bsz),
        in_specs=in_specs,
        out_specs=out_specs,
        out_shape=out_shape,
        scratch_shapes=[
            pltpu.VMEM((bsz, 1, D_LRU), F32),
            pltpu.VMEM((bsz, SUBLANES, D_LRU), F32),
            pltpu.VMEM((tm + SUBLANES, D_LRU), F32),
            pltpu.VMEM((tm, D_LRU), F32),
            pltpu.VMEM((tm, D_LRU), F32),
            pltpu.VMEM((tm, D_LRU), F32),
        ],
        compiler_params=pltpu.CompilerParams(
            dimension_semantics=("arbitrary", "arbitrary"), vmem_limit_bytes=VMEM_LIMIT),
        name="prompt_pre_ln0" if apply_ln0 else "prompt_pre",
    )(x, mod, ln0_g, ln0_b, lw["w_in"], lw["conv_w"], lw["conv_b"], lw["w_gates"], lw["b_a"],
      lw["b_x"], lw["lam"], lw["q_norm_g"], lw["w_uq"], lw["kv_norm_g"], lw["w_kv"],
      lw["g_lru"], cq_tab, sq_tab, ck_tab, sk_tab)


ATTN_HEADS_PER_STEP = 4


def _attn_kernel(tq, q_ref, k_ref, v_ref, o_ref):
    qi = pl.program_id(2)
    heads = range(ATTN_HEADS_PER_STEP)
    lanes = lambda hh: slice(hh * HEAD_PAD, (hh + 1) * HEAD_PAD)
    qs = [q_ref[0, :, lanes(hh)] for hh in heads]

    def tile_step(j, carry, masked):
        off = pl.multiple_of(j * tq, tq)
        scores, probs, out = {}, {}, {}

        def score(hh):
            kt = k_ref[0, pl.ds(off, tq), lanes(hh)]
            s = lax.dot_general(qs[hh], kt, (((1,), (1,)), ((), ())),
                                preferred_element_type=F32)
            if masked:
                row = lax.broadcasted_iota(jnp.int32, (tq, tq), 0)
                col = lax.broadcasted_iota(jnp.int32, (tq, tq), 1)
                s = jnp.where(col <= row, s, -jnp.inf)
            scores[hh] = s

        def softmax(hh):
            m_prev = carry[hh][0]
            m_new = jnp.maximum(m_prev, jnp.max(scores[hh], axis=1, keepdims=True))
            probs[hh] = (m_new, jnp.exp2(m_prev - m_new),
                         jnp.exp2(scores[hh] - m_new).astype(BF16))

        def weigh(hh):
            m_new, alpha, p = probs[hh]
            vt = v_ref[0, pl.ds(off, tq), lanes(hh)]
            out[hh] = (m_new, alpha * carry[hh][1] + _bdot(p, vt))

        n = len(heads)
        for t in range(n + 2):
            if t < n:
                score(t)
            if 0 <= t - 1 < n:
                softmax(t - 1)
            if 0 <= t - 2 < n:
                weigh(t - 2)
        return tuple(out[hh] for hh in heads)

    init = tuple((jnp.full((tq, 1), -jnp.inf, F32), jnp.zeros((tq, HEAD_PAD), F32))
                 for _ in heads)
    carry = lax.fori_loop(0, qi, lambda j, c: tile_step(j, c, False), init)
    final = tile_step(qi, carry, True)
    low_half = lax.broadcasted_iota(jnp.int32, (tq, HEAD_PAD), 1) < D_V
    normed = [final[hh][1] / final[hh][1][:, D_V:D_V + 1] for hh in heads]
    for pair in range(ATTN_HEADS_PER_STEP // 2):
        o_ref[0, :, lanes(pair)] = jnp.where(
            low_half, normed[2 * pair], pltpu.roll(normed[2 * pair + 1], D_V, axis=1))


def _prompt_attention(q, k, v):
    bsz, seq, _ = q.shape
    tq = _prompt_tile(seq)
    width = ATTN_HEADS_PER_STEP * HEAD_PAD
    out_width = ATTN_HEADS_PER_STEP * D_V
    return pl.pallas_call(
        functools.partial(_attn_kernel, tq),
        grid=(bsz, N_HEADS // ATTN_HEADS_PER_STEP, seq // tq),
        in_specs=[
            pl.BlockSpec((1, tq, width), lambda b, h, i: (b, i, h)),
            pl.BlockSpec((1, seq, width), lambda b, h, i: (b, 0, h)),
            pl.BlockSpec((1, seq, width), lambda b, h, i: (b, 0, h)),
        ],
        out_specs=pl.BlockSpec((1, tq, out_width), lambda b, h, i: (b, i, h)),
        out_shape=jax.ShapeDtypeStruct((bsz, seq, N_HEADS * D_V), F32),
        compiler_params=pltpu.CompilerParams(
            dimension_semantics=("arbitrary", "arbitrary", "arbitrary"),
            vmem_limit_bytes=VMEM_LIMIT),
        name="prompt_attention",
    )(q, k, v)


def _post_kernel(from_latent, ff_chunk, x_ref, mod_ref, mixl_ref, att_ref, wuv_ref, gatt_ref,
                 wo_ref, lng_ref, lnb_ref, wup_ref, wdn_ref, o_ref):
    x = x_ref[0]
    if from_latent:
        lat = jnp.concatenate([att_ref[hh] for hh in range(N_HEADS)], axis=1)
        ya = _bdot(lat, wuv_ref[...])
    else:
        ya = att_ref[0]
    na = _rms_norm(ya, gatt_ref[...], N_HEADS * D_V).astype(BF16)
    o = _bdot(jnp.concatenate([mixl_ref[0], na], axis=1), wo_ref[...])
    x1 = _layer_norm(ALPHA * x + mod_ref[2, 0] * o, lng_ref[0:1, :], lnb_ref[0:1, :])
    h2 = (x1 * (1.0 + mod_ref[4, 0]) + mod_ref[3, 0]).astype(BF16)
    f = jnp.zeros(x.shape, F32)
    for c in range(D_FF // ff_chunk):
        up = _bdot(h2, wup_ref[:, c * ff_chunk:(c + 1) * ff_chunk])
        act = jnp.square(jnp.maximum(up, 0.0)).astype(BF16)
        f = f + _bdot(act, wdn_ref[c * ff_chunk:(c + 1) * ff_chunk, :])
    o_ref[0] = _layer_norm(ALPHA * x1 + mod_ref[5, 0] * f, lng_ref[1:2, :], lnb_ref[1:2, :])


def _post(from_latent, x, mod, mixl, att, lw):
    groups, rows, _ = x.shape
    tm = _prompt_tile(rows)
    tile = lambda w: pl.BlockSpec((1, tm, w), lambda g, s: (g, s, 0))
    mod_rows = mod.shape[2]
    if from_latent:
        att_spec = pl.BlockSpec((N_HEADS, tm, KV_RANK), lambda g, s: (0, s, 0))
    else:
        att_spec = tile(N_HEADS * D_V)
    return pl.pallas_call(
        functools.partial(_post_kernel, from_latent, 1024),
        grid=(groups, rows // tm),
        in_specs=[
            tile(D_MODEL),
            pl.BlockSpec((6, 1, mod_rows, D_MODEL), lambda g, s: (0, g, 0, 0)),
            tile(D_LRU), att_spec,
            _const_spec(lw["w_uv_heads"].shape), _const_spec((1, N_HEADS * D_V)),
            _const_spec(lw["w_o"].shape),
            _const_spec((2, D_MODEL)), _const_spec((2, D_MODEL)),
            _const_spec(lw["w_up"].shape), _const_spec(lw["w_down"].shape),
        ],
        out_specs=tile(D_MODEL),
        out_shape=jax.ShapeDtypeStruct(x.shape, F32),
        compiler_params=pltpu.CompilerParams(
            dimension_semantics=("arbitrary", "arbitrary"), vmem_limit_bytes=VMEM_LIMIT),
        name="sample_post" if from_latent else "prompt_post",
    )(x, mod, mixl, att, lw["w_uv_heads"], lw["g_att"], lw["w_o"],
      lw["ln_g"], lw["ln_b"], lw["w_up"], lw["w_down"])


def _sample_pre_kernel(apply_ln0, *refs):
    (x_ref, mod_ref, ln0g_ref, ln0b_ref, win_ref, convw_ref, convb_ref, wg_ref, ba_ref,
     bx_ref, lam_ref, qg_ref, wuq_ref, kvg_ref, wukt_ref, glru_ref,
     cq_ref, sq_ref, ck_ref, sk_ref, h0_ref, buf_ref) = refs[:22]
    outs = refs[22:]
    if apply_ln0:
        xn_ref, outs = outs[0], outs[1:]
    q_ref, qlat_ref, mixl_ref, ckv_ref, kr_ref, hnew_ref, conv_ref = outs

    x = x_ref[...]
    if apply_ln0:
        x = _layer_norm(x, ln0g_ref[...], ln0b_ref[...])
        xn_ref[...] = x
    h = (x * (1.0 + mod_ref[1, 0]) + mod_ref[0, 0]).astype(BF16)
    proj = _bdot(h, win_ref[...])
    u_raw = proj[:, OFF_U:OFF_GATE]
    gate = proj[:, OFF_GATE:OFF_CQ]

    uc = convb_ref[...] + u_raw * convw_ref[CONV_W - 1:CONV_W, :]
    for kk in range(CONV_W - 1):
        uc = uc + buf_ref[kk] * convw_ref[kk:kk + 1, :]
    for kk in range(CONV_W - 2):
        conv_ref[kk] = buf_ref[kk + 1]
    conv_ref[CONV_W - 2] = u_raw

    a, bb = _lru_coeffs(uc, wg_ref, ba_ref, bx_ref, lam_ref)
    h_new = a * h0_ref[...] + bb
    hnew_ref[...] = h_new
    y = h_new * _gelu_tanh(gate)
    mixl_ref[...] = _rms_norm(y, glru_ref[...], D_LRU).astype(BF16)

    q, ckvn, krw = _latent_heads(proj[:, OFF_CQ:], cq_ref[0:1, :], sq_ref[0:1, :],
                                 ck_ref[0:1, :], sk_ref[0:1, :], qg_ref, wuq_ref, kvg_ref)
    qb = q.astype(BF16)
    q_ref[...] = qb
    ckv_ref[...] = ckvn
    kr_ref[...] = krw[:, OFF_KR_LANE:OFF_KR_LANE + D_ROPE]
    for hh in range(N_HEADS):
        qlat_ref[hh] = _bdot(qb[:, hh * HEAD_PAD:(hh + 1) * HEAD_PAD], wukt_ref[hh])


def _sample_pre(apply_ln0, x, mod, ln0_g, ln0_b, lw, tabs, h0, buf):
    rows = x.shape[0]
    cq_tab, sq_tab, ck_tab, sk_tab = tabs
    full = lambda a: _const_spec(a.shape)
    args = (x, mod, ln0_g, ln0_b, lw["w_in"], lw["conv_w"], lw["conv_b"], lw["w_gates"],
            lw["b_a"], lw["b_x"], lw["lam"], lw["q_norm_g"], lw["w_uq"], lw["kv_norm_g"],
            lw["w_uk_t"], lw["g_lru"], cq_tab, sq_tab, ck_tab, sk_tab, h0, buf)
    out_shape = [
        jax.ShapeDtypeStruct((rows, D_HEADS_PAD), BF16),
        jax.ShapeDtypeStruct((N_HEADS, rows, KV_RANK), F32),
        jax.ShapeDtypeStruct((rows, D_LRU), BF16),
        jax.ShapeDtypeStruct((rows, KV_RANK), F32),
        jax.ShapeDtypeStruct((rows, D_ROPE), F32),
        jax.ShapeDtypeStruct((rows, D_LRU), F32),
        jax.ShapeDtypeStruct((CONV_W - 1, rows, D_LRU), F32),
    ]
    if apply_ln0:
        out_shape = [jax.ShapeDtypeStruct((rows, D_MODEL), F32)] + out_shape
    return pl.pallas_call(
        functools.partial(_sample_pre_kernel, apply_ln0),
        grid=(1,),
        in_specs=[full(a) for a in args],
        out_specs=[_const_spec(s.shape) for s in out_shape],
        out_shape=out_shape,
        compiler_params=pltpu.CompilerParams(
            dimension_semantics=("arbitrary",), vmem_limit_bytes=VMEM_LIMIT),
        name="sample_pre_ln0" if apply_ln0 else "sample_pre",
    )(*args)


PAGED_SLOTS = 4


def _paged_attn_kernel(layer, pages_per_chunk, chunks_per_seq, page_size,
                       pt_ref, qlat_ref, qrope_ref, cnew_ref, knew_ref, cache_c, cache_kt,
                       o_ref, *scratch):
    cbufs = scratch[:PAGED_SLOTS]
    kbufs = scratch[PAGED_SLOTS:2 * PAGED_SLOTS]
    sem_c, sem_k, m_sc, l_sc, acc_sc = scratch[2 * PAGED_SLOTS:]
    g = pl.program_id(0)
    n_steps = pl.num_programs(0)
    total_chunks = PAGED_SLOTS * n_steps
    bufs = tuple(zip(cbufs, kbufs))
    ahead = PAGED_SLOTS - 1

    def page_copies(chunk_idx, sl, j):
        seq = chunk_idx // chunks_per_seq
        page = pt_ref[seq, (chunk_idx % chunks_per_seq) * pages_per_chunk + j]
        span = pl.ds(j * page_size, page_size)
        cb, kb = bufs[sl]
        return (pltpu.make_async_copy(cache_c.at[layer, page], cb.at[span, :], sem_c.at[sl]),
                pltpu.make_async_copy(cache_kt.at[layer, page], kb.at[:, span], sem_k.at[sl]))

    def start_chunk(chunk_idx, sl):
        for j in range(pages_per_chunk):
            for cp in page_copies(chunk_idx, sl, j):
                cp.start()

    def wait_chunk(chunk_idx, sl):
        for j in range(pages_per_chunk):
            for cp in page_copies(chunk_idx, sl, j):
                cp.wait()

    qlat = qlat_ref[0]
    qrope = qrope_ref[0]

    def consume(sl, m_prev, l_prev, acc_prev):
        cb, kb = bufs[sl]
        ck = cb[...].astype(BF16)
        s = (lax.dot_general(qlat, ck, (((1,), (1,)), ((), ())), preferred_element_type=F32)
             + _bdot(qrope, kb[...].astype(BF16)))
        m_new = jnp.maximum(m_prev, jnp.max(s, axis=1, keepdims=True))
        alpha = jnp.exp2(m_prev - m_new)
        p = jnp.exp2(s - m_new)
        l_new = alpha * l_prev + jnp.sum(p, axis=1, keepdims=True)
        acc = alpha * acc_prev + _bdot(p.astype(BF16), ck)
        return m_new, l_new, acc

    first = PAGED_SLOTS * g

    @pl.when(g == 0)
    def _():
        for k in range(ahead):
            start_chunk(k, k)

    cnew = cnew_ref[0].astype(BF16).astype(F32)
    knew = knew_ref[0].astype(BF16).astype(F32)
    s_new = (jnp.sum(qlat.astype(F32) * cnew, axis=1, keepdims=True)
             + jnp.sum(qrope.astype(F32) * knew, axis=1, keepdims=True))
    opens = lax.rem(first, chunks_per_seq) == 0
    state = (jnp.where(opens, s_new, m_sc[...]),
             jnp.where(opens, jnp.ones_like(s_new), l_sc[...]),
             jnp.where(opens, jnp.broadcast_to(cnew, acc_sc.shape), acc_sc[...]))

    for k in range(PAGED_SLOTS):
        wait_chunk(first + k, k)
        state = consume(k, *state)
        start_chunk(lax.rem(first + k + ahead, total_chunks), (k + ahead) % PAGED_SLOTS)

    m_fin, l_fin, acc_fin = state
    m_sc[...] = m_fin
    l_sc[...] = l_fin
    acc_sc[...] = acc_fin
    o_ref[0] = acc_fin / l_fin

    @pl.when(g == n_steps - 1)
    def _():
        for k in range(ahead):
            wait_chunk(k, k)


def _paged_pages_per_chunk(n_pages):
    for p in (32, 16, 8, 4, 2, 1):
        if n_pages % (PAGED_SLOTS * p) == 0:
            return p
    raise ValueError(f"the page count per sequence ({n_pages}) must be a multiple of {PAGED_SLOTS}")


def _paged_attention(layer, page_table, qlat, qrope, cnew, knew, cache_ckv, cache_krope_t):
    nseq, n_pages = page_table.shape
    page_size = cache_ckv.shape[2]
    ppc = _paged_pages_per_chunk(n_pages)
    chunks_per_seq = n_pages // ppc
    steps_per_seq = chunks_per_seq // PAGED_SLOTS
    per_seq = lambda w: pl.BlockSpec((1, N_HEADS, w), lambda s, pt: (s // steps_per_seq, 0, 0))
    one_row = lambda w: pl.BlockSpec((1, 1, w), lambda s, pt: (s // steps_per_seq, 0, 0))
    grid_spec = pltpu.PrefetchScalarGridSpec(
        num_scalar_prefetch=1,
        grid=(nseq * steps_per_seq,),
        in_specs=[
            per_seq(KV_RANK), per_seq(D_ROPE), one_row(KV_RANK), one_row(D_ROPE),
            pl.BlockSpec(memory_space=pl.ANY), pl.BlockSpec(memory_space=pl.ANY),
        ],
        out_specs=per_seq(KV_RANK),
        scratch_shapes=(
            [pltpu.VMEM((ppc * page_size, KV_RANK), F32)] * PAGED_SLOTS
            + [pltpu.VMEM((D_ROPE, ppc * page_size), F32)] * PAGED_SLOTS
        ) + [
            pltpu.SemaphoreType.DMA((PAGED_SLOTS,)),
            pltpu.SemaphoreType.DMA((PAGED_SLOTS,)),
            pltpu.VMEM((N_HEADS, 1), F32),
            pltpu.VMEM((N_HEADS, 1), F32),
            pltpu.VMEM((N_HEADS, KV_RANK), F32),
        ],
    )
    return pl.pallas_call(
        functools.partial(_paged_attn_kernel, layer, ppc, chunks_per_seq, page_size),
        grid_spec=grid_spec,
        out_shape=jax.ShapeDtypeStruct((nseq, N_HEADS, KV_RANK), F32),
        compiler_params=pltpu.CompilerParams(
            dimension_semantics=("arbitrary",), vmem_limit_bytes=VMEM_LIMIT),
        name="paged_attention",
    )(page_table, qlat, qrope, cnew, knew, cache_ckv, cache_krope_t)


def _pad_heads(w, width):
    pad = [(0, 0)] * (w.ndim - 1) + [(0, HEAD_PAD - width)]
    return jnp.pad(w, pad).reshape(*w.shape[:-2], D_HEADS_PAD)


def _block_diag_chunks(w):
    per = GATE_CHUNK // LRU_BLOCK
    eye = jnp.eye(per, dtype=w.dtype)
    w = w.reshape(D_LRU // GATE_CHUNK, per, LRU_BLOCK, LRU_BLOCK)
    return jnp.einsum("cnij,nm->cnimj", w, eye).reshape(-1, GATE_CHUNK, GATE_CHUNK)


def _layer_weights(l, p):
    w_in = p["w_in"][l]
    kr_cols = w_in[:, OFF_KRA:OFF_KRA + D_ROPE]
    kr_swap = jnp.concatenate([-kr_cols[:, HALF_ROPE:], kr_cols[:, :HALF_ROPE]], axis=1)
    lane_pad = lambda w: jnp.pad(w, ((0, 0), (OFF_KR_LANE, LANES - OFF_KR_LANE - w.shape[1])))
    w_in_pad = jnp.concatenate([w_in[:, :OFF_KRA], lane_pad(kr_cols), lane_pad(kr_swap)], axis=1)

    w_uq = p["w_uq"][l]
    x1 = w_uq[..., D_NOPE:D_NOPE + HALF_ROPE]
    x2 = w_uq[..., D_NOPE + HALF_ROPE:]
    w_uq_swap = jnp.concatenate([jnp.zeros_like(w_uq[..., :D_NOPE]), -x2, x1], axis=-1)
    w_uq_pad = jnp.concatenate([_pad_heads(w_uq, D_QK), _pad_heads(w_uq_swap, D_QK)], axis=1)

    w_uk = p["w_uk"][l]
    w_uv = p["w_uv"][l]
    w_kv = jnp.concatenate([_pad_heads(w_uk, D_NOPE), _pad_heads(w_uv, D_V)], axis=1)

    w_uk_t = jnp.pad(jnp.transpose(w_uk, (1, 2, 0)), ((0, 0), (0, HEAD_PAD - D_NOPE), (0, 0)))
    w_uv_heads = jnp.einsum("rhv,hg->hrgv", w_uv, jnp.eye(N_HEADS, dtype=w_uv.dtype)).reshape(
        N_HEADS * KV_RANK, N_HEADS * D_V)
    row = lambda v: v.reshape(1, -1)
    return {
        "w_in": w_in_pad.astype(BF16),
        "conv_w": p["conv_w"][l], "conv_b": row(p["conv_b"][l]),
        "w_gates": jnp.stack([_block_diag_chunks(p["w_a"][l]),
                              _block_diag_chunks(p["w_x"][l])]).astype(BF16),
        "b_a": row(p["b_a"][l]), "b_x": row(p["b_x"][l]), "lam": row(p["lru_lambda"][l]),
        "q_norm_g": row(p["q_norm_g"][l]), "w_uq": w_uq_pad.astype(BF16),
        "kv_norm_g": row(p["kv_norm_g"][l]), "w_kv": w_kv.astype(BF16),
        "w_uk_t": w_uk_t.astype(BF16), "w_uv_heads": w_uv_heads.astype(BF16),
        "g_lru": row(p["g_lru"][l]), "g_att": row(p["g_att"][l]),
        "w_o": p["w_o"][l].astype(BF16),
        "ln_g": p["ln_g"][l], "ln_b": p["ln_b"][l],
        "w_up": p["w_up"][l].astype(BF16), "w_down": p["w_down"][l].astype(BF16),
    }


def kernel(x_prompt, x_sample, cache_ckv, cache_krope, state_lru_h, state_conv, page_table,
           c_prompt, c_sample, ln0_g, ln0_b, w_ada, b_ada, w_in, conv_w, conv_b, w_a, b_a,
           w_x, b_x, lru_lambda, q_norm_g, w_uq, kv_norm_g, w_uk, w_uv, g_lru, g_att, w_o,
           ln_g, ln_b, w_up, w_down):
    params = dict(w_in=w_in, conv_w=conv_w, conv_b=conv_b, w_a=w_a, b_a=b_a, w_x=w_x, b_x=b_x,
                  lru_lambda=lru_lambda, q_norm_g=q_norm_g, w_uq=w_uq, kv_norm_g=kv_norm_g,
                  w_uk=w_uk, w_uv=w_uv, g_lru=g_lru, g_att=g_att, w_o=w_o, ln_g=ln_g,
                  ln_b=ln_b, w_up=w_up, w_down=w_down)
    b_p, s_p, _ = x_prompt.shape
    b_s, s_s, _ = x_sample.shape
    if s_s != 1:
        raise ValueError("the sample group is a single-token decode step")
    depth = w_ada.shape[0]
    past_len = page_table.shape[1] * cache_ckv.shape[2]

    mods = _ada_mod(jnp.concatenate([c_prompt, c_sample], axis=0), w_ada, b_ada)
    tabs_p = _rope_tables(s_p, 0)
    tabs_s = _rope_tables(SUBLANES, past_len)
    ln0g, ln0b = ln0_g.reshape(1, -1), ln0_b.reshape(1, -1)
    cache_krope_t = jnp.swapaxes(cache_krope, 2, 3)

    xp = x_prompt
    xs = x_sample.reshape(b_s, D_MODEL)
    outs_p = [[] for _ in range(4)]
    outs_s = [[] for _ in range(4)]
    for l in range(depth):
        lw = _layer_weights(l, params)
        first = l == 0
        mod_p = mods[l, :, :b_p].reshape(6, b_p, 1, D_MODEL)
        mod_s = mods[l, :, b_p:].reshape(6, 1, b_s, D_MODEL)

        res = _prompt_pre(first, xp, mod_p, ln0g, ln0b, lw, tabs_p)
        if first:
            xp, res = res[0], res[1:]
        q, k, v, mixl, ckv, kr, h_last, conv = res
        att = _prompt_attention(q, k, v)
        xp = _post(False, xp, mod_p, mixl, att, lw)
        for dst, val in zip(outs_p, (ckv, kr, h_last[-1].reshape(b_p, D_LRU), conv[-1])):
            dst.append(val)

        res = _sample_pre(first, xs, mod_s, ln0g, ln0b, lw, tabs_s, state_lru_h[l],
                          jnp.transpose(state_conv[l], (1, 0, 2)))
        if first:
            xs, res = res[0], res[1:]
        q_s, qlat, mixl_s, ckv_s, kr_s, h_new, conv_s = res
        qlat_b = jnp.transpose(qlat, (1, 0, 2)).astype(BF16)
        qrope_b = q_s.reshape(b_s, N_HEADS, HEAD_PAD)[:, :, D_NOPE:D_QK]
        o_lat = _paged_attention(l, page_table, qlat_b, qrope_b, ckv_s.reshape(b_s, 1, KV_RANK),
                                 kr_s.reshape(b_s, 1, D_ROPE), cache_ckv, cache_krope_t)
        o_lat_h = jnp.transpose(o_lat, (1, 0, 2)).astype(BF16)
        xs = _post(True, xs.reshape(1, b_s, D_MODEL), mod_s, mixl_s.reshape(1, b_s, D_LRU),
                   o_lat_h, lw).reshape(b_s, D_MODEL)
        for dst, val in zip(outs_s, (ckv_s.reshape(b_s, 1, KV_RANK), kr_s.reshape(b_s, 1, D_ROPE),
                                     h_new, jnp.transpose(conv_s, (1, 0, 2)))):
            dst.append(val)

    stack = lambda vals: jnp.stack(vals, axis=0)
    return (xp, xs.reshape(b_s, 1, D_MODEL),
            stack(outs_p[0]), stack(outs_p[1]), stack(outs_p[2]), stack(outs_p[3]),
            stack(outs_s[0]), stack(outs_s[1]), stack(outs_s[2]), stack(outs_s[3]))
```

```python
import functools
import math

import jax
import jax.numpy as jnp
import numpy as np
from jax import lax
from jax.experimental import pallas as pl
from jax.experimental.pallas import tpu as pltpu

D_MODEL = 1024
D_LRU = 512
LRU_BLOCKS = 8
LRU_BLOCK = D_LRU // LRU_BLOCKS
CONV_W = 4
LRU_C = 8.0
N_HEADS = 8
D_NOPE = 64
D_ROPE = 32
D_QK = D_NOPE + D_ROPE
D_V = 64
Q_RANK = 384
KV_RANK = 256
ROPE_THETA = 10000.0
D_FF = 4 * D_MODEL
DEPTH = 2
ALPHA = (2 * DEPTH) ** 0.25
ATT_SCALE = D_QK ** -0.5
EPS = 1e-6

LANES = 128
SUBLANES = 8
VMEM_BYTES_V7X = 64 * 1024 * 1024
VMEM_LIMIT = VMEM_BYTES_V7X - 8 * 1024 * 1024

HEAD_PAD = LANES
D_HEADS_PAD = N_HEADS * HEAD_PAD
HALF_ROPE = D_ROPE // 2
OFF_U, OFF_GATE, OFF_CQ, OFF_CKV = 0, D_LRU, 2 * D_LRU, 2 * D_LRU + Q_RANK
OFF_KRA = OFF_CKV + KV_RANK
OFF_KRB = OFF_KRA + LANES
D_IN_PAD = OFF_KRB + LANES
EXP2_SCALE = ATT_SCALE * math.log2(math.e)
GATE_CHUNK = 256
OFF_KR_LANE = D_NOPE

F32 = jnp.float32
BF16 = jnp.bfloat16


def _prompt_tile(seq):
    for t in (512, 256, 128, 64, 32, 16, 8):
        if seq % t == 0:
            return t
    raise ValueError(f"sequence length {seq} must be a multiple of 8")


def _const_spec(shape):
    nd = len(shape)
    return pl.BlockSpec(shape, lambda *_: (0,) * nd, pipeline_mode=pl.Buffered(1))


def _layer_norm(x, g, b):
    mu = jnp.mean(x, axis=-1, keepdims=True)
    xc = x - mu
    var = jnp.mean(xc * xc, axis=-1, keepdims=True)
    return xc * lax.rsqrt(var + EPS) * g + b


def _rms_norm(x, g, n):
    ms = jnp.sum(x * x, axis=-1, keepdims=True) * (1.0 / n)
    return x * lax.rsqrt(ms + EPS) * g


def _sigmoid(x):
    return 1.0 / (1.0 + jnp.exp(-x))


def _softplus(x):
    return jnp.maximum(x, 0.0) + jnp.log1p(jnp.exp(-jnp.abs(x)))


def _gelu_tanh(x):
    c = math.sqrt(2.0 / math.pi)
    return 0.5 * x * (1.0 + jnp.tanh(c * (x + 0.044715 * (x * x * x))))


def _bdot(a, b):
    return jnp.dot(a, b, preferred_element_type=F32)


def _lru_coeffs(uc, wg_ref, ba_ref, bx_ref, lam_ref):
    ub = uc.astype(BF16)
    chunks = [ub[:, c * GATE_CHUNK:(c + 1) * GATE_CHUNK] for c in range(D_LRU // GATE_CHUNK)]
    gate = lambda n: jnp.concatenate(
        [_bdot(x, wg_ref[n, c]) for c, x in enumerate(chunks)], axis=1)
    r = _sigmoid(gate(0) + ba_ref[...])
    i = _sigmoid(gate(1) + bx_ref[...])
    log_a = (-LRU_C) * r * _softplus(-lam_ref[...])
    a = jnp.exp(log_a)
    t = jnp.tanh(log_a)
    b = jnp.sqrt((-2.0 * t) / (1.0 - t)) * (i * uc)
    return a, b


def _latent_heads(proj, cq_tab, sq_tab, ck_tab, sk_tab, qg_ref, wuq_ref, kvg_ref):
    cut = lambda lo, hi: proj[:, lo - OFF_CQ:hi - OFF_CQ]
    cqn = _rms_norm(cut(OFF_CQ, OFF_CKV), qg_ref[...], Q_RANK)
    qq = _bdot(cqn.astype(BF16), wuq_ref[...])
    q = qq[:, :D_HEADS_PAD] * cq_tab + qq[:, D_HEADS_PAD:] * sq_tab
    ckvn = _rms_norm(cut(OFF_CKV, OFF_KRA), kvg_ref[...], KV_RANK)
    krw = cut(OFF_KRA, OFF_KRB) * ck_tab + cut(OFF_KRB, D_IN_PAD) * sk_tab
    return q, ckvn, krw


def _ada_kernel(c_ref, w_ref, b_ref, o_ref):
    c = c_ref[...]
    s = (c * _sigmoid(c)).astype(BF16)
    o_ref[0, 0] = _bdot(s, w_ref[0].astype(BF16)) + b_ref[0]


def _ada_mod(c_all, w_ada, b_ada):
    rows = c_all.shape[0]
    depth, _, n_out = w_ada.shape
    n_mod = n_out // D_MODEL
    return pl.pallas_call(
        _ada_kernel,
        grid=(depth, n_mod),
        in_specs=[
            pl.BlockSpec((rows, D_MODEL), lambda l, j: (0, 0)),
            pl.BlockSpec((1, D_MODEL, D_MODEL), lambda l, j: (l, 0, j)),
            pl.BlockSpec((1, 1, D_MODEL), lambda l, j: (l, 0, j)),
        ],
        out_specs=pl.BlockSpec((1, 1, rows, D_MODEL), lambda l, j: (l, j, 0, 0)),
        out_shape=jax.ShapeDtypeStruct((depth, n_mod, rows, D_MODEL), F32),
        compiler_params=pltpu.CompilerParams(
            dimension_semantics=("arbitrary", "arbitrary"), vmem_limit_bytes=VMEM_LIMIT),
        name="ada_mod",
    )(c_all, w_ada, b_ada.reshape(depth, 1, n_out))


def _rope_table_kernel(pos0, inv_ref, one_ref, rope_ref, cq_ref, sq_ref, ck_ref, sk_ref):
    rows = ck_ref.shape[0]
    pos = (lax.broadcasted_iota(jnp.int32, (rows, 1), 0)
           + (pl.program_id(0) * rows + pos0)).astype(F32)
    ang = pos * inv_ref[...]
    cos = rope_ref[...] * jnp.cos(ang)
    sin = rope_ref[...] * jnp.sin(ang)
    ck_ref[...] = cos
    sk_ref[...] = sin
    cq_ref[...] = jnp.concatenate([(cos + one_ref[...]) * EXP2_SCALE] * N_HEADS, axis=1)
    sq_ref[...] = jnp.concatenate([sin * EXP2_SCALE] * N_HEADS, axis=1)


def _rope_tables(rows, pos0):
    inv = ROPE_THETA ** (-jnp.arange(HALF_ROPE, dtype=F32) / HALF_ROPE)
    zeros = lambda n: jnp.zeros((n,), F32)
    ones = lambda n: jnp.ones((n,), F32)
    head_inv = jnp.concatenate([zeros(D_NOPE), inv, inv, zeros(HEAD_PAD - D_QK)])[None, :]
    head_rope = jnp.concatenate([zeros(D_NOPE), ones(D_ROPE), zeros(HEAD_PAD - D_QK)])[None, :]
    head_one = jnp.concatenate([ones(D_NOPE), zeros(HEAD_PAD - D_NOPE)])[None, :]
    tr = _prompt_tile(rows)
    widths = (D_HEADS_PAD, D_HEADS_PAD, LANES, LANES)
    return pl.pallas_call(
        functools.partial(_rope_table_kernel, pos0),
        grid=(rows // tr,),
        in_specs=[_const_spec((1, LANES))] * 3,
        out_specs=[pl.BlockSpec((tr, w), lambda i: (i, 0)) for w in widths],
        out_shape=[jax.ShapeDtypeStruct((rows, w), F32) for w in widths],
        compiler_params=pltpu.CompilerParams(dimension_semantics=("arbitrary",)),
        name="rope_tables",
    )(head_inv, head_one, head_rope)


def _prompt_pre_kernel(apply_ln0, tm, *refs):
    (x_ref, mod_ref, ln0g_ref, ln0b_ref, win_ref, convw_ref, convb_ref, wg_ref, ba_ref,
     bx_ref, lam_ref, qg_ref, wuq_ref, kvg_ref, wkv_ref, glru_ref,
     cq_ref, sq_ref, ck_ref, sk_ref) = refs[:20]
    outs = refs[20:]
    if apply_ln0:
        xn_ref, outs = outs[0], outs[1:]
    (q_ref, k_ref, v_ref, mixl_ref, ckv_ref, kr_ref, hlast_ref, conv_ref,
     hcar_ref, halo_ref, ubuf_ref, a_sc, b_sc, hs_sc) = outs

    s_idx = pl.program_id(0)
    b_idx = pl.program_id(1)

    @pl.when(s_idx == 0)
    def _():
        halo_ref[b_idx] = jnp.zeros((SUBLANES, D_LRU), F32)
        hcar_ref[b_idx] = jnp.zeros((1, D_LRU), F32)

    x = x_ref[0]
    if apply_ln0:
        x = _layer_norm(x, ln0g_ref[...], ln0b_ref[...])
        xn_ref[0] = x
    h = (x * (1.0 + mod_ref[1, 0]) + mod_ref[0, 0]).astype(BF16)
    proj = _bdot(h, win_ref[...])
    u_raw = proj[:, OFF_U:OFF_GATE]
    gate = proj[:, OFF_GATE:OFF_CQ]

    q, ckvn, krw = _latent_heads(proj[:, OFF_CQ:], cq_ref[...], sq_ref[...], ck_ref[...],
                                 sk_ref[...], qg_ref, wuq_ref, kvg_ref)
    q_ref[0] = q.astype(BF16)
    ckv_ref[0] = ckvn
    kr_ref[0] = krw[:, OFF_KR_LANE:OFF_KR_LANE + D_ROPE]
    kv = _bdot(ckvn.astype(BF16), wkv_ref[...])
    k_ref[0] = (kv[:, :D_HEADS_PAD] + jnp.concatenate([krw] * N_HEADS, axis=1)).astype(BF16)
    lane = lax.broadcasted_iota(jnp.int32, (1, D_HEADS_PAD), 1) & (HEAD_PAD - 1)
    v_ref[0] = (kv[:, D_HEADS_PAD:] + (lane == D_V).astype(F32)).astype(BF16)

    ubuf_ref[0:SUBLANES, :] = halo_ref[b_idx]
    ubuf_ref[SUBLANES:, :] = u_raw
    halo_ref[b_idx] = u_raw[tm - SUBLANES:, :]
    uc = convb_ref[...] + u_raw * convw_ref[CONV_W - 1:CONV_W, :]
    for kk in range(CONV_W - 1):
        off = SUBLANES - (CONV_W - 1) + kk
        uc = uc + ubuf_ref[off:off + tm, :] * convw_ref[kk:kk + 1, :]
    conv_ref[0, 0] = ubuf_ref[tm + SUBLANES - (CONV_W - 1):tm + SUBLANES, :]

    a, bb = _lru_coeffs(uc, wg_ref, ba_ref, bx_ref, lam_ref)

    groups = (tm // SUBLANES, SUBLANES, D_LRU)
    a = a.reshape(groups)
    bb = bb.reshape(groups)
    row8 = lax.broadcasted_iota(jnp.int32, (1, SUBLANES, D_LRU), 1)
    for d in (1, 2, 4):
        a_sh = pltpu.roll(a, d, axis=1)
        b_sh = pltpu.roll(bb, d, axis=1)
        take = row8 >= d
        bb = jnp.where(take, a * b_sh + bb, bb)
        a = jnp.where(take, a * a_sh, a)
    a_sc[...] = a.reshape(tm, D_LRU)
    b_sc[...] = bb.reshape(tm, D_LRU)
    gate_act = _gelu_tanh(gate)

    def group_step(g, hprev):
        off = pl.multiple_of(g * SUBLANES, SUBLANES)
        hg = a_sc[pl.ds(off, SUBLANES), :] * hprev + b_sc[pl.ds(off, SUBLANES), :]
        hs_sc[pl.ds(off, SUBLANES), :] = hg
        return hg[SUBLANES - 1:SUBLANES, :]

    h_fin = lax.fori_loop(0, tm // SUBLANES, group_step, hcar_ref[b_idx], unroll=8)
    hcar_ref[b_idx] = h_fin
    hlast_ref[0, 0] = h_fin

    y = hs_sc[...] * gate_act
    mixl_ref[0] = _rms_norm(y, glru_ref[...], D_LRU).astype(BF16)


def _prompt_pre(apply_ln0, x, mod, ln0_g, ln0_b, lw, tabs):
    bsz, seq, _ = x.shape
    tm = _prompt_tile(seq)
    ns = seq // tm
    cq_tab, sq_tab, ck_tab, sk_tab = tabs
    tile = lambda w: pl.BlockSpec((1, tm, w), lambda s, b: (b, s, 0))
    in_specs = [
        tile(D_MODEL),
        pl.BlockSpec((6, 1, 1, D_MODEL), lambda s, b: (0, b, 0, 0)),
        _const_spec((1, D_MODEL)), _const_spec((1, D_MODEL)),
        _const_spec(lw["w_in"].shape), _const_spec((CONV_W, D_LRU)), _const_spec((1, D_LRU)),
        _const_spec(lw["w_gates"].shape), _const_spec((1, D_LRU)), _const_spec((1, D_LRU)),
        _const_spec((1, D_LRU)), _const_spec((1, Q_RANK)), _const_spec(lw["w_uq"].shape),
        _const_spec((1, KV_RANK)), _const_spec(lw["w_kv"].shape), _const_spec((1, D_LRU)),
        pl.BlockSpec((tm, D_HEADS_PAD), lambda s, b: (s, 0)),
        pl.BlockSpec((tm, D_HEADS_PAD), lambda s, b: (s, 0)),
        pl.BlockSpec((tm, LANES), lambda s, b: (s, 0)),
        pl.BlockSpec((tm, LANES), lambda s, b: (s, 0)),
    ]
    out_specs = [
        tile(D_HEADS_PAD), tile(D_HEADS_PAD), tile(D_HEADS_PAD), tile(D_LRU), tile(KV_RANK),
        tile(D_ROPE),
        pl.BlockSpec((1, 1, 1, D_LRU), lambda s, b: (s, b, 0, 0)),
        pl.BlockSpec((1, 1, CONV_W - 1, D_LRU), lambda s, b: (s, b, 0, 0)),
    ]
    out_shape = [
        jax.ShapeDtypeStruct((bsz, seq, D_HEADS_PAD), BF16),
        jax.ShapeDtypeStruct((bsz, seq, D_HEADS_PAD), BF16),
        jax.ShapeDtypeStruct((bsz, seq, D_HEADS_PAD), BF16),
        jax.ShapeDtypeStruct((bsz, seq, D_LRU), BF16),
        jax.ShapeDtypeStruct((bsz, seq, KV_RANK), F32),
        jax.ShapeDtypeStruct((bsz, seq, D_ROPE), F32),
        jax.ShapeDtypeStruct((ns, bsz, 1, D_LRU), F32),
        jax.ShapeDtypeStruct((ns, bsz, CONV_W - 1, D_LRU), F32),
    ]
    if apply_ln0:
        out_specs = [tile(D_MODEL)] + out_specs
        out_shape = [jax.ShapeDtypeStruct((bsz, seq, D_MODEL), F32)] + out_shape
    return pl.pallas_call(
        functools.partial(_prompt_pre_kernel, apply_ln0, tm),
        grid=(ns, bsz),
        in_specs=in_specs,
        out_specs=out_specs,
        out_shape=out_shape,
        scratch_shapes=[
            pltpu.VMEM((bsz, 1, D_LRU), F32),
            pltpu.VMEM((bsz, SUBLANES, D_LRU), F32),
            pltpu.VMEM((tm + SUBLANES, D_LRU), F32),
            pltpu.VMEM((tm, D_LRU), F32),
            pltpu.VMEM((tm, D_LRU), F32),
            pltpu.VMEM((tm, D_LRU), F32),
        ],
        compiler_params=pltpu.CompilerParams(
            dimension_semantics=("arbitrary", "arbitrary"), vmem_limit_bytes=VMEM_LIMIT),
        name="prompt_pre_ln0" if apply_ln0 else "prompt_pre",
    )(x, mod, ln0_g, ln0_b, lw["w_in"], lw["conv_w"], lw["conv_b"], lw["w_gates"], lw["b_a"],
      lw["b_x"], lw["lam"], lw["q_norm_g"], lw["w_uq"], lw["kv_norm_g"], lw["w_kv"],
      lw["g_lru"], cq_tab, sq_tab, ck_tab, sk_tab)


ATTN_HEADS_PER_STEP = 4


def _attn_kernel(tq, q_ref, k_ref, v_ref, o_ref):
    qi = pl.program_id(2)
    heads = range(ATTN_HEADS_PER_STEP)
    lanes = lambda hh: slice(hh * HEAD_PAD, (hh + 1) * HEAD_PAD)
    qs = [q_ref[0, :, lanes(hh)] for hh in heads]

    def tile_step(j, carry, masked):
        off = pl.multiple_of(j * tq, tq)
        scores, probs, out = {}, {}, {}

        def score(hh):
            kt = k_ref[0, pl.ds(off, tq), lanes(hh)]
            s = lax.dot_general(qs[hh], kt, (((1,), (1,)), ((), ())),
                                preferred_element_type=F32)
            if masked:
                row = lax.broadcasted_iota(jnp.int32, (tq, tq), 0)
                col = lax.broadcasted_iota(jnp.int32, (tq, tq), 1)
                s = jnp.where(col <= row, s, -jnp.inf)
            scores[hh] = s

        def softmax(hh):
            m_prev = carry[hh][0]
            m_new = jnp.maximum(m_prev, jnp.max(scores[hh], axis=1, keepdims=True))
            probs[hh] = (m_new, jnp.exp2(m_prev - m_new),
                         jnp.exp2(scores[hh] - m_new).astype(BF16))

        def weigh(hh):
            m_new, alpha, p = probs[hh]
            vt = v_ref[0, pl.ds(off, tq), lanes(hh)]
            out[hh] = (m_new, alpha * carry[hh][1] + _bdot(p, vt))

        n = len(heads)
        for t in range(n + 2):
            if t < n:
                score(t)
            if 0 <= t - 1 < n:
                softmax(t - 1)
            if 0 <= t - 2 < n:
                weigh(t - 2)
        return tuple(out[hh] for hh in heads)

    init = tuple((jnp.full((tq, 1), -jnp.inf, F32), jnp.zeros((tq, HEAD_PAD), F32))
                 for _ in heads)
    carry = lax.fori_loop(0, qi, lambda j, c: tile_step(j, c, False), init)
    final = tile_step(qi, carry, True)
    low_half = lax.broadcasted_iota(jnp.int32, (tq, HEAD_PAD), 1) < D_V
    normed = [final[hh][1] / final[hh][1][:, D_V:D_V + 1] for hh in heads]
    for pair in range(ATTN_HEADS_PER_STEP // 2):
        o_ref[0, :, lanes(pair)] = jnp.where(
            low_half, normed[2 * pair], pltpu.roll(normed[2 * pair + 1], D_V, axis=1))


def _prompt_attention(q, k, v):
    bsz, seq, _ = q.shape
    tq = _prompt_tile(seq)
    width = ATTN_HEADS_PER_STEP * HEAD_PAD
    out_width = ATTN_HEADS_PER_STEP * D_V
    return pl.pallas_call(
        functools.partial(_attn_kernel, tq),
        grid=(bsz, N_HEADS // ATTN_HEADS_PER_STEP, seq // tq),
        in_specs=[
            pl.BlockSpec((1, tq, width), lambda b, h, i: (b, i, h)),
            pl.BlockSpec((1, seq, width), lambda b, h, i: (b, 0, h)),
            pl.BlockSpec((1, seq, width), lambda b, h, i: (b, 0, h)),
        ],
        out_specs=pl.BlockSpec((1, tq, out_width), lambda b, h, i: (b, i, h)),
        out_shape=jax.ShapeDtypeStruct((bsz, seq, N_HEADS * D_V), F32),
        compiler_params=pltpu.CompilerParams(
            dimension_semantics=("arbitrary", "arbitrary", "arbitrary"),
            vmem_limit_bytes=VMEM_LIMIT),
        name="prompt_attention",
    )(q, k, v)


def _post_kernel(from_latent, ff_chunk, x_ref, mod_ref, mixl_ref, att_ref, wuv_ref, gatt_ref,
                 wo_ref, lng_ref, lnb_ref, wup_ref, wdn_ref, o_ref):
    tm = x_ref.shape[1]
    n_half = 2 if tm % 32 == 0 else 1
    hr = tm // n_half
    rows = [slice(i * hr, (i + 1) * hr) for i in range(n_half)]
    per_row = mod_ref.shape[2] != 1

    def mod(k, r):
        return mod_ref[k, 0][r] if per_row else mod_ref[k, 0]

    def project(r):
        if from_latent:
            lat = jnp.concatenate([att_ref[hh, r, :] for hh in range(N_HEADS)], axis=1)
            ya = _bdot(lat, wuv_ref[...])
        else:
            ya = att_ref[0, r, :]
        na = _rms_norm(ya, gatt_ref[...], N_HEADS * D_V).astype(BF16)
        return _bdot(jnp.concatenate([mixl_ref[0, r, :], na], axis=1), wo_ref[...])

    def norm1(r, o):
        x1 = _layer_norm(ALPHA * x_ref[0, r, :] + mod(2, r) * o, lng_ref[0:1, :], lnb_ref[0:1, :])
        return x1, (x1 * (1.0 + mod(4, r)) + mod(3, r)).astype(BF16)

    def mlp(h2):
        f = jnp.zeros((hr, D_MODEL), F32)
        for c in range(D_FF // ff_chunk):
            up = _bdot(h2, wup_ref[:, c * ff_chunk:(c + 1) * ff_chunk])
            act = jnp.square(jnp.maximum(up, 0.0)).astype(BF16)
            f = f + _bdot(act, wdn_ref[c * ff_chunk:(c + 1) * ff_chunk, :])
        return f

    def norm2(r, x1, f):
        o_ref[0, r, :] = _layer_norm(ALPHA * x1 + mod(5, r) * f, lng_ref[1:2, :], lnb_ref[1:2, :])

    outs = [project(r) for r in rows]
    mids = {}
    mids[0] = norm1(rows[0], outs[0])
    ffs = {0: mlp(mids[0][1])}
    for i in range(1, n_half):
        mids[i] = norm1(rows[i], outs[i])
        ffs[i] = mlp(mids[i][1])
        norm2(rows[i - 1], mids[i - 1][0], ffs[i - 1])
    norm2(rows[n_half - 1], mids[n_half - 1][0], ffs[n_half - 1])


def _post(from_latent, x, mod, mixl, att, lw):
    groups, rows, _ = x.shape
    tm = _prompt_tile(rows)
    tile = lambda w: pl.BlockSpec((1, tm, w), lambda g, s: (g, s, 0))
    mod_rows = mod.shape[2]
    if from_latent:
        att_spec = pl.BlockSpec((N_HEADS, tm, KV_RANK), lambda g, s: (0, s, 0))
    else:
        att_spec = tile(N_HEADS * D_V)
    return pl.pallas_call(
        functools.partial(_post_kernel, from_latent, 1024),
        grid=(groups, rows // tm),
        in_specs=[
            tile(D_MODEL),
            pl.BlockSpec((6, 1, mod_rows, D_MODEL), lambda g, s: (0, g, 0, 0)),
            tile(D_LRU), att_spec,
            _const_spec(lw["w_uv_heads"].shape), _const_spec((1, N_HEADS * D_V)),
            _const_spec(lw["w_o"].shape),
            _const_spec((2, D_MODEL)), _const_spec((2, D_MODEL)),
            _const_spec(lw["w_up"].shape), _const_spec(lw["w_down"].shape),
        ],
        out_specs=tile(D_MODEL),
        out_shape=jax.ShapeDtypeStruct(x.shape, F32),
        compiler_params=pltpu.CompilerParams(
            dimension_semantics=("arbitrary", "arbitrary"), vmem_limit_bytes=VMEM_LIMIT),
        name="sample_post" if from_latent else "prompt_post",
    )(x, mod, mixl, att, lw["w_uv_heads"], lw["g_att"], lw["w_o"],
      lw["ln_g"], lw["ln_b"], lw["w_up"], lw["w_down"])


def _sample_pre_kernel(apply_ln0, *refs):
    (x_ref, mod_ref, ln0g_ref, ln0b_ref, win_ref, convw_ref, convb_ref, wg_ref, ba_ref,
     bx_ref, lam_ref, qg_ref, wuq_ref, kvg_ref, wukt_ref, glru_ref,
     cq_ref, sq_ref, ck_ref, sk_ref, h0_ref, buf_ref) = refs[:22]
    outs = refs[22:]
    if apply_ln0:
        xn_ref, outs = outs[0], outs[1:]
    q_ref, qlat_ref, mixl_ref, ckv_ref, kr_ref, hnew_ref, conv_ref = outs

    x = x_ref[...]
    if apply_ln0:
        x = _layer_norm(x, ln0g_ref[...], ln0b_ref[...])
        xn_ref[...] = x
    h = (x * (1.0 + mod_ref[1, 0]) + mod_ref[0, 0]).astype(BF16)
    proj = _bdot(h, win_ref[...])
    u_raw = proj[:, OFF_U:OFF_GATE]
    gate = proj[:, OFF_GATE:OFF_CQ]

    uc = convb_ref[...] + u_raw * convw_ref[CONV_W - 1:CONV_W, :]
    for kk in range(CONV_W - 1):
        uc = uc + buf_ref[kk] * convw_ref[kk:kk + 1, :]
    for kk in range(CONV_W - 2):
        conv_ref[kk] = buf_ref[kk + 1]
    conv_ref[CONV_W - 2] = u_raw

    a, bb = _lru_coeffs(uc, wg_ref, ba_ref, bx_ref, lam_ref)
    h_new = a * h0_ref[...] + bb
    hnew_ref[...] = h_new
    y = h_new * _gelu_tanh(gate)
    mixl_ref[...] = _rms_norm(y, glru_ref[...], D_LRU).astype(BF16)

    q, ckvn, krw = _latent_heads(proj[:, OFF_CQ:], cq_ref[0:1, :], sq_ref[0:1, :],
                                 ck_ref[0:1, :], sk_ref[0:1, :], qg_ref, wuq_ref, kvg_ref)
    qb = q.astype(BF16)
    q_ref[...] = qb
    ckv_ref[...] = ckvn
    kr_ref[...] = krw[:, OFF_KR_LANE:OFF_KR_LANE + D_ROPE]
    for hh in range(N_HEADS):
        qlat_ref[hh] = _bdot(qb[:, hh * HEAD_PAD:(hh + 1) * HEAD_PAD], wukt_ref[hh])


def _sample_pre(apply_ln0, x, mod, ln0_g, ln0_b, lw, tabs, h0, buf):
    rows = x.shape[0]
    cq_tab, sq_tab, ck_tab, sk_tab = tabs
    full = lambda a: _const_spec(a.shape)
    args = (x, mod, ln0_g, ln0_b, lw["w_in"], lw["conv_w"], lw["conv_b"], lw["w_gates"],
            lw["b_a"], lw["b_x"], lw["lam"], lw["q_norm_g"], lw["w_uq"], lw["kv_norm_g"],
            lw["w_uk_t"], lw["g_lru"], cq_tab, sq_tab, ck_tab, sk_tab, h0, buf)
    out_shape = [
        jax.ShapeDtypeStruct((rows, D_HEADS_PAD), BF16),
        jax.ShapeDtypeStruct((N_HEADS, rows, KV_RANK), F32),
        jax.ShapeDtypeStruct((rows, D_LRU), BF16),
        jax.ShapeDtypeStruct((rows, KV_RANK), F32),
        jax.ShapeDtypeStruct((rows, D_ROPE), F32),
        jax.ShapeDtypeStruct((rows, D_LRU), F32),
        jax.ShapeDtypeStruct((CONV_W - 1, rows, D_LRU), F32),
    ]
    if apply_ln0:
        out_shape = [jax.ShapeDtypeStruct((rows, D_MODEL), F32)] + out_shape
    return pl.pallas_call(
        functools.partial(_sample_pre_kernel, apply_ln0),
        grid=(1,),
        in_specs=[full(a) for a in args],
        out_specs=[_const_spec(s.shape) for s in out_shape],
        out_shape=out_shape,
        compiler_params=pltpu.CompilerParams(
            dimension_semantics=("arbitrary",), vmem_limit_bytes=VMEM_LIMIT),
        name="sample_pre_ln0" if apply_ln0 else "sample_pre",
    )(*args)


PAGED_SLOTS = 4


def _paged_attn_kernel(layer, pages_per_chunk, chunks_per_seq, page_size,
                       pt_ref, qlat_ref, qrope_ref, cnew_ref, knew_ref, cache_c, cache_kt,
                       o_ref, *scratch):
    cbufs = scratch[:PAGED_SLOTS]
    kbufs = scratch[PAGED_SLOTS:2 * PAGED_SLOTS]
    sem_c, sem_k, m_sc, l_sc, acc_sc = scratch[2 * PAGED_SLOTS:]
    g = pl.program_id(0)
    n_steps = pl.num_programs(0)
    total_chunks = PAGED_SLOTS * n_steps
    bufs = tuple(zip(cbufs, kbufs))
    ahead = PAGED_SLOTS - 1

    def page_copies(chunk_idx, sl, j):
        seq = chunk_idx // chunks_per_seq
        page = pt_ref[seq, (chunk_idx % chunks_per_seq) * pages_per_chunk + j]
        span = pl.ds(j * page_size, page_size)
        cb, kb = bufs[sl]
        return (pltpu.make_async_copy(cache_c.at[layer, page], cb.at[span, :], sem_c.at[sl]),
                pltpu.make_async_copy(cache_kt.at[layer, page], kb.at[:, span], sem_k.at[sl]))

    def start_chunk(chunk_idx, sl):
        for j in range(pages_per_chunk):
            for cp in page_copies(chunk_idx, sl, j):
                cp.start()

    def wait_chunk(chunk_idx, sl):
        for j in range(pages_per_chunk):
            for cp in page_copies(chunk_idx, sl, j):
                cp.wait()

    qlat = qlat_ref[0]
    qrope = qrope_ref[0]

    def consume(sl, m_prev, l_prev, acc_prev):
        cb, kb = bufs[sl]
        ck = cb[...].astype(BF16)
        s = (lax.dot_general(qlat, ck, (((1,), (1,)), ((), ())), preferred_element_type=F32)
             + _bdot(qrope, kb[...].astype(BF16)))
        m_new = jnp.maximum(m_prev, jnp.max(s, axis=1, keepdims=True))
        alpha = jnp.exp2(m_prev - m_new)
        p = jnp.exp2(s - m_new)
        l_new = alpha * l_prev + jnp.sum(p, axis=1, keepdims=True)
        acc = alpha * acc_prev + _bdot(p.astype(BF16), ck)
        return m_new, l_new, acc

    first = PAGED_SLOTS * g

    @pl.when(g == 0)
    def _():
        for k in range(ahead):
            start_chunk(k, k)

    cnew = cnew_ref[0].astype(BF16).astype(F32)
    knew = knew_ref[0].astype(BF16).astype(F32)
    s_new = (jnp.sum(qlat.astype(F32) * cnew, axis=1, keepdims=True)
             + jnp.sum(qrope.astype(F32) * knew, axis=1, keepdims=True))
    opens = lax.rem(first, chunks_per_seq) == 0
    state = (jnp.where(opens, s_new, m_sc[...]),
             jnp.where(opens, jnp.ones_like(s_new), l_sc[...]),
             jnp.where(opens, jnp.broadcast_to(cnew, acc_sc.shape), acc_sc[...]))

    for k in range(PAGED_SLOTS):
        wait_chunk(first + k, k)
        state = consume(k, *state)
        start_chunk(lax.rem(first + k + ahead, total_chunks), (k + ahead) % PAGED_SLOTS)

    m_fin, l_fin, acc_fin = state
    m_sc[...] = m_fin
    l_sc[...] = l_fin
    acc_sc[...] = acc_fin
    o_ref[0] = acc_fin / l_fin

    @pl.when(g == n_steps - 1)
    def _():
        for k in range(ahead):
            wait_chunk(k, k)


def _paged_pages_per_chunk(n_pages):
    for p in (32, 16, 8, 4, 2, 1):
        if n_pages % (PAGED_SLOTS * p) == 0:
            return p
    raise ValueError(f"the page count per sequence ({n_pages}) must be a multiple of {PAGED_SLOTS}")


def _paged_attention(layer, page_table, qlat, qrope, cnew, knew, cache_ckv, cache_krope_t):
    nseq, n_pages = page_table.shape
    page_size = cache_ckv.shape[2]
    ppc = _paged_pages_per_chunk(n_pages)
    chunks_per_seq = n_pages // ppc
    steps_per_seq = chunks_per_seq // PAGED_SLOTS
    per_seq = lambda w: pl.BlockSpec((1, N_HEADS, w), lambda s, pt: (s // steps_per_seq, 0, 0))
    one_row = lambda w: pl.BlockSpec((1, 1, w), lambda s, pt: (s // steps_per_seq, 0, 0))
    grid_spec = pltpu.PrefetchScalarGridSpec(
        num_scalar_prefetch=1,
        grid=(nseq * steps_per_seq,),
        in_specs=[
            per_seq(KV_RANK), per_seq(D_ROPE), one_row(KV_RANK), one_row(D_ROPE),
            pl.BlockSpec(memory_space=pl.ANY), pl.BlockSpec(memory_space=pl.ANY),
        ],
        out_specs=per_seq(KV_RANK),
        scratch_shapes=(
            [pltpu.VMEM((ppc * page_size, KV_RANK), F32)] * PAGED_SLOTS
            + [pltpu.VMEM((D_ROPE, ppc * page_size), F32)] * PAGED_SLOTS
        ) + [
            pltpu.SemaphoreType.DMA((PAGED_SLOTS,)),
            pltpu.SemaphoreType.DMA((PAGED_SLOTS,)),
            pltpu.VMEM((N_HEADS, 1), F32),
            pltpu.VMEM((N_HEADS, 1), F32),
            pltpu.VMEM((N_HEADS, KV_RANK), F32),
        ],
    )
    return pl.pallas_call(
        functools.partial(_paged_attn_kernel, layer, ppc, chunks_per_seq, page_size),
        grid_spec=grid_spec,
        out_shape=jax.ShapeDtypeStruct((nseq, N_HEADS, KV_RANK), F32),
        compiler_params=pltpu.CompilerParams(
            dimension_semantics=("arbitrary",), vmem_limit_bytes=VMEM_LIMIT),
        name="paged_attention",
    )(page_table, qlat, qrope, cnew, knew, cache_ckv, cache_krope_t)


def _pad_heads(w, width):
    pad = [(0, 0)] * (w.ndim - 1) + [(0, HEAD_PAD - width)]
    return jnp.pad(w, pad).reshape(*w.shape[:-2], D_HEADS_PAD)


def _block_diag_chunks(w):
    per = GATE_CHUNK // LRU_BLOCK
    eye = jnp.eye(per, dtype=w.dtype)
    w = w.reshape(D_LRU // GATE_CHUNK, per, LRU_BLOCK, LRU_BLOCK)
    return jnp.einsum("cnij,nm->cnimj", w, eye).reshape(-1, GATE_CHUNK, GATE_CHUNK)


def _layer_weights(l, p):
    w_in = p["w_in"][l]
    kr_cols = w_in[:, OFF_KRA:OFF_KRA + D_ROPE]
    kr_swap = jnp.concatenate([-kr_cols[:, HALF_ROPE:], kr_cols[:, :HALF_ROPE]], axis=1)
    lane_pad = lambda w: jnp.pad(w, ((0, 0), (OFF_KR_LANE, LANES - OFF_KR_LANE - w.shape[1])))
    w_in_pad = jnp.concatenate([w_in[:, :OFF_KRA], lane_pad(kr_cols), lane_pad(kr_swap)], axis=1)

    w_uq = p["w_uq"][l]
    x1 = w_uq[..., D_NOPE:D_NOPE + HALF_ROPE]
    x2 = w_uq[..., D_NOPE + HALF_ROPE:]
    w_uq_swap = jnp.concatenate([jnp.zeros_like(w_uq[..., :D_NOPE]), -x2, x1], axis=-1)
    w_uq_pad = jnp.concatenate([_pad_heads(w_uq, D_QK), _pad_heads(w_uq_swap, D_QK)], axis=1)

    w_uk = p["w_uk"][l]
    w_uv = p["w_uv"][l]
    w_kv = jnp.concatenate([_pad_heads(w_uk, D_NOPE), _pad_heads(w_uv, D_V)], axis=1)

    w_uk_t = jnp.pad(jnp.transpose(w_uk, (1, 2, 0)), ((0, 0), (0, HEAD_PAD - D_NOPE), (0, 0)))
    w_uv_heads = jnp.einsum("rhv,hg->hrgv", w_uv, jnp.eye(N_HEADS, dtype=w_uv.dtype)).reshape(
        N_HEADS * KV_RANK, N_HEADS * D_V)
    row = lambda v: v.reshape(1, -1)
    return {
        "w_in": w_in_pad.astype(BF16),
        "conv_w": p["conv_w"][l], "conv_b": row(p["conv_b"][l]),
        "w_gates": jnp.stack([_block_diag_chunks(p["w_a"][l]),
                              _block_diag_chunks(p["w_x"][l])]).astype(BF16),
        "b_a": row(p["b_a"][l]), "b_x": row(p["b_x"][l]), "lam": row(p["lru_lambda"][l]),
        "q_norm_g": row(p["q_norm_g"][l]), "w_uq": w_uq_pad.astype(BF16),
        "kv_norm_g": row(p["kv_norm_g"][l]), "w_kv": w_kv.astype(BF16),
        "w_uk_t": w_uk_t.astype(BF16), "w_uv_heads": w_uv_heads.astype(BF16),
        "g_lru": row(p["g_lru"][l]), "g_att": row(p["g_att"][l]),
        "w_o": p["w_o"][l].astype(BF16),
        "ln_g": p["ln_g"][l], "ln_b": p["ln_b"][l],
        "w_up": p["w_up"][l].astype(BF16), "w_down": p["w_down"][l].astype(BF16),
    }


def kernel(x_prompt, x_sample, cache_ckv, cache_krope, state_lru_h, state_conv, page_table,
           c_prompt, c_sample, ln0_g, ln0_b, w_ada, b_ada, w_in, conv_w, conv_b, w_a, b_a,
           w_x, b_x, lru_lambda, q_norm_g, w_uq, kv_norm_g, w_uk, w_uv, g_lru, g_att, w_o,
           ln_g, ln_b, w_up, w_down):
    params = dict(w_in=w_in, conv_w=conv_w, conv_b=conv_b, w_a=w_a, b_a=b_a, w_x=w_x, b_x=b_x,
                  lru_lambda=lru_lambda, q_norm_g=q_norm_g, w_uq=w_uq, kv_norm_g=kv_norm_g,
                  w_uk=w_uk, w_uv=w_uv, g_lru=g_lru, g_att=g_att, w_o=w_o, ln_g=ln_g,
                  ln_b=ln_b, w_up=w_up, w_down=w_down)
    b_p, s_p, _ = x_prompt.shape
    b_s, s_s, _ = x_sample.shape
    if s_s != 1:
        raise ValueError("the sample group is a single-token decode step")
    depth = w_ada.shape[0]
    past_len = page_table.shape[1] * cache_ckv.shape[2]

    mods = _ada_mod(jnp.concatenate([c_prompt, c_sample], axis=0), w_ada, b_ada)
    tabs_p = _rope_tables(s_p, 0)
    tabs_s = _rope_tables(SUBLANES, past_len)
    ln0g, ln0b = ln0_g.reshape(1, -1), ln0_b.reshape(1, -1)
    cache_krope_t = jnp.swapaxes(cache_krope, 2, 3)

    xp = x_prompt
    xs = x_sample.reshape(b_s, D_MODEL)
    outs_p = [[] for _ in range(4)]
    outs_s = [[] for _ in range(4)]
    for l in range(depth):
        lw = _layer_weights(l, params)
        first = l == 0
        mod_p = mods[l, :, :b_p].reshape(6, b_p, 1, D_MODEL)
        mod_s = mods[l, :, b_p:].reshape(6, 1, b_s, D_MODEL)

        res = _prompt_pre(first, xp, mod_p, ln0g, ln0b, lw, tabs_p)
        if first:
            xp, res = res[0], res[1:]
        q, k, v, mixl, ckv, kr, h_last, conv = res
        att = _prompt_attention(q, k, v)
        xp = _post(False, xp, mod_p, mixl, att, lw)
        for dst, val in zip(outs_p, (ckv, kr, h_last[-1].reshape(b_p, D_LRU), conv[-1])):
            dst.append(val)

        res = _sample_pre(first, xs, mod_s, ln0g, ln0b, lw, tabs_s, state_lru_h[l],
                          jnp.transpose(state_conv[l], (1, 0, 2)))
        if first:
            xs, res = res[0], res[1:]
        q_s, qlat, mixl_s, ckv_s, kr_s, h_new, conv_s = res
        qlat_b = jnp.transpose(qlat, (1, 0, 2)).astype(BF16)
        qrope_b = q_s.reshape(b_s, N_HEADS, HEAD_PAD)[:, :, D_NOPE:D_QK]
        o_lat = _paged_attention(l, page_table, qlat_b, qrope_b, ckv_s.reshape(b_s, 1, KV_RANK),
                                 kr_s.reshape(b_s, 1, D_ROPE), cache_ckv, cache_krope_t)
        o_lat_h = jnp.transpose(o_lat, (1, 0, 2)).astype(BF16)
        xs = _post(True, xs.reshape(1, b_s, D_MODEL), mod_s, mixl_s.reshape(1, b_s, D_LRU),
                   o_lat_h, lw).reshape(b_s, D_MODEL)
        for dst, val in zip(outs_s, (ckv_s.reshape(b_s, 1, KV_RANK), kr_s.reshape(b_s, 1, D_ROPE),
                                     h_new, jnp.transpose(conv_s, (1, 0, 2)))):
            dst.append(val)

    stack = lambda vals: jnp.stack(vals, axis=0)
    return (xp, xs.reshape(b_s, 1, D_MODEL),
            stack(outs_p[0]), stack(outs_p[1]), stack(outs_p[2]), stack(outs_p[3]),
            stack(outs_s[0]), stack(outs_s[1]), stack(outs_s[2]), stack(outs_s[3]))
```

```python
import functools
import math

import jax
import jax.numpy as jnp
import numpy as np
from jax import lax
from jax.experimental import pallas as pl
from jax.experimental.pallas import tpu as pltpu

D_MODEL = 1024
D_LRU = 512
LRU_BLOCKS = 8
LRU_BLOCK = D_LRU // LRU_BLOCKS
CONV_W = 4
LRU_C = 8.0
N_HEADS = 8
D_NOPE = 64
D_ROPE = 32
D_QK = D_NOPE + D_ROPE
D_V = 64
Q_RANK = 384
KV_RANK = 256
ROPE_THETA = 10000.0
D_FF = 4 * D_MODEL
DEPTH = 2
ALPHA = (2 * DEPTH) ** 0.25
ATT_SCALE = D_QK ** -0.5
EPS = 1e-6

LANES = 128
SUBLANES = 8
VMEM_BYTES_V7X = 64 * 1024 * 1024
VMEM_LIMIT = VMEM_BYTES_V7X - 8 * 1024 * 1024

HEAD_PAD = LANES
D_HEADS_PAD = N_HEADS * HEAD_PAD
HALF_ROPE = D_ROPE // 2
OFF_U, OFF_GATE, OFF_CQ, OFF_CKV = 0, D_LRU, 2 * D_LRU, 2 * D_LRU + Q_RANK
OFF_KRA = OFF_CKV + KV_RANK
OFF_KRB = OFF_KRA + LANES
D_IN_PAD = OFF_KRB + LANES
EXP2_SCALE = ATT_SCALE * math.log2(math.e)
GATE_CHUNK = 256
OFF_KR_LANE = D_NOPE

F32 = jnp.float32
BF16 = jnp.bfloat16


def _prompt_tile(seq):
    for t in (512, 256, 128, 64, 32, 16, 8):
        if seq % t == 0:
            return t
    raise ValueError(f"sequence length {seq} must be a multiple of 8")


def _const_spec(shape):
    nd = len(shape)
    return pl.BlockSpec(shape, lambda *_: (0,) * nd, pipeline_mode=pl.Buffered(1))


def _layer_norm(x, g, b):
    mu = jnp.mean(x, axis=-1, keepdims=True)
    xc = x - mu
    var = jnp.mean(xc * xc, axis=-1, keepdims=True)
    return xc * lax.rsqrt(var + EPS) * g + b


def _rms_norm(x, g, n):
    ms = jnp.sum(x * x, axis=-1, keepdims=True) * (1.0 / n)
    return x * lax.rsqrt(ms + EPS) * g


def _sigmoid(x):
    return 1.0 / (1.0 + jnp.exp(-x))


def _softplus(x):
    return jnp.maximum(x, 0.0) + jnp.log1p(jnp.exp(-jnp.abs(x)))


def _gelu_tanh(x):
    c = math.sqrt(2.0 / math.pi)
    return 0.5 * x * (1.0 + jnp.tanh(c * (x + 0.044715 * (x * x * x))))


def _bdot(a, b):
    return jnp.dot(a, b, preferred_element_type=F32)


def _lru_coeffs(uc, wg_ref, ba_ref, bx_ref, lam_ref):
    ub = uc.astype(BF16)
    chunks = [ub[:, c * GATE_CHUNK:(c + 1) * GATE_CHUNK] for c in range(D_LRU // GATE_CHUNK)]
    gate = lambda n: jnp.concatenate(
        [_bdot(x, wg_ref[n, c]) for c, x in enumerate(chunks)], axis=1)
    r = _sigmoid(gate(0) + ba_ref[...])
    i = _sigmoid(gate(1) + bx_ref[...])
    log_a = (-LRU_C) * r * _softplus(-lam_ref[...])
    a = jnp.exp(log_a)
    t = jnp.tanh(log_a)
    b = jnp.sqrt((-2.0 * t) / (1.0 - t)) * (i * uc)
    return a, b


def _latent_heads(proj, cq_tab, sq_tab, ck_tab, sk_tab, qg_ref, wuq_ref, kvg_ref):
    cut = lambda lo, hi: proj[:, lo - OFF_CQ:hi - OFF_CQ]
    cqn = _rms_norm(cut(OFF_CQ, OFF_CKV), qg_ref[...], Q_RANK)
    qq = _bdot(cqn.astype(BF16), wuq_ref[...])
    q = qq[:, :D_HEADS_PAD] * cq_tab + qq[:, D_HEADS_PAD:] * sq_tab
    ckvn = _rms_norm(cut(OFF_CKV, OFF_KRA), kvg_ref[...], KV_RANK)
    krw = cut(OFF_KRA, OFF_KRB) * ck_tab + cut(OFF_KRB, D_IN_PAD) * sk_tab
    return q, ckvn, krw


def _ada_kernel(c_ref, w_ref, b_ref, o_ref):
    c = c_ref[...]
    s = (c * _sigmoid(c)).astype(BF16)
    o_ref[0, 0] = _bdot(s, w_ref[0].astype(BF16)) + b_ref[0]


def _ada_mod(c_all, w_ada, b_ada):
    rows = c_all.shape[0]
    depth, _, n_out = w_ada.shape
    n_mod = n_out // D_MODEL
    return pl.pallas_call(
        _ada_kernel,
        grid=(depth, n_mod),
        in_specs=[
            pl.BlockSpec((rows, D_MODEL), lambda l, j: (0, 0)),
            pl.BlockSpec((1, D_MODEL, D_MODEL), lambda l, j: (l, 0, j)),
            pl.BlockSpec((1, 1, D_MODEL), lambda l, j: (l, 0, j)),
        ],
        out_specs=pl.BlockSpec((1, 1, rows, D_MODEL), lambda l, j: (l, j, 0, 0)),
        out_shape=jax.ShapeDtypeStruct((depth, n_mod, rows, D_MODEL), F32),
        compiler_params=pltpu.CompilerParams(
            dimension_semantics=("arbitrary", "arbitrary"), vmem_limit_bytes=VMEM_LIMIT),
        name="ada_mod",
    )(c_all, w_ada, b_ada.reshape(depth, 1, n_out))


def _rope_table_kernel(pos0, inv_ref, one_ref, rope_ref, cq_ref, sq_ref, ck_ref, sk_ref):
    rows = ck_ref.shape[0]
    pos = (lax.broadcasted_iota(jnp.int32, (rows, 1), 0)
           + (pl.program_id(0) * rows + pos0)).astype(F32)
    ang = pos * inv_ref[...]
    cos = rope_ref[...] * jnp.cos(ang)
    sin = rope_ref[...] * jnp.sin(ang)
    ck_ref[...] = cos
    sk_ref[...] = sin
    cq_ref[...] = jnp.concatenate([(cos + one_ref[...]) * EXP2_SCALE] * N_HEADS, axis=1)
    sq_ref[...] = jnp.concatenate([sin * EXP2_SCALE] * N_HEADS, axis=1)


def _rope_tables(rows, pos0):
    inv = ROPE_THETA ** (-jnp.arange(HALF_ROPE, dtype=F32) / HALF_ROPE)
    zeros = lambda n: jnp.zeros((n,), F32)
    ones = lambda n: jnp.ones((n,), F32)
    head_inv = jnp.concatenate([zeros(D_NOPE), inv, inv, zeros(HEAD_PAD - D_QK)])[None, :]
    head_rope = jnp.concatenate([zeros(D_NOPE), ones(D_ROPE), zeros(HEAD_PAD - D_QK)])[None, :]
    head_one = jnp.concatenate([ones(D_NOPE), zeros(HEAD_PAD - D_NOPE)])[None, :]
    tr = _prompt_tile(rows)
    widths = (D_HEADS_PAD, D_HEADS_PAD, LANES, LANES)
    return pl.pallas_call(
        functools.partial(_rope_table_kernel, pos0),
        grid=(rows // tr,),
        in_specs=[_const_spec((1, LANES))] * 3,
        out_specs=[pl.BlockSpec((tr, w), lambda i: (i, 0)) for w in widths],
        out_shape=[jax.ShapeDtypeStruct((rows, w), F32) for w in widths],
        compiler_params=pltpu.CompilerParams(dimension_semantics=("arbitrary",)),
        name="rope_tables",
    )(head_inv, head_one, head_rope)


def _prompt_pre_kernel(apply_ln0, tm, *refs):
    (x_ref, mod_ref, ln0g_ref, ln0b_ref, win_ref, convw_ref, convb_ref, wg_ref, ba_ref,
     bx_ref, lam_ref, qg_ref, wuq_ref, kvg_ref, wkv_ref, glru_ref,
     cq_ref, sq_ref, ck_ref, sk_ref) = refs[:20]
    outs = refs[20:]
    if apply_ln0:
        xn_ref, outs = outs[0], outs[1:]
    (q_ref, k_ref, v_ref, mixl_ref, ckv_ref, kr_ref, hlast_ref, conv_ref,
     hcar_ref, halo_ref, ubuf_ref, a_sc, b_sc, hs_sc) = outs

    s_idx = pl.program_id(0)
    b_idx = pl.program_id(1)

    @pl.when(s_idx == 0)
    def _():
        halo_ref[b_idx] = jnp.zeros((SUBLANES, D_LRU), F32)
        hcar_ref[b_idx] = jnp.zeros((1, D_LRU), F32)

    x = x_ref[0]
    if apply_ln0:
        x = _layer_norm(x, ln0g_ref[...], ln0b_ref[...])
        xn_ref[0] = x
    h = (x * (1.0 + mod_ref[1, 0]) + mod_ref[0, 0]).astype(BF16)
    proj = _bdot(h, win_ref[...])
    u_raw = proj[:, OFF_U:OFF_GATE]
    gate = proj[:, OFF_GATE:OFF_CQ]

    q, ckvn, krw = _latent_heads(proj[:, OFF_CQ:], cq_ref[...], sq_ref[...], ck_ref[...],
                                 sk_ref[...], qg_ref, wuq_ref, kvg_ref)
    q_ref[0] = q.astype(BF16)
    ckv_ref[0] = ckvn
    kr_ref[0] = krw[:, OFF_KR_LANE:OFF_KR_LANE + D_ROPE]
    kv = _bdot(ckvn.astype(BF16), wkv_ref[...])
    k_ref[0] = (kv[:, :D_HEADS_PAD] + jnp.concatenate([krw] * N_HEADS, axis=1)).astype(BF16)
    lane = lax.broadcasted_iota(jnp.int32, (1, D_HEADS_PAD), 1) & (HEAD_PAD - 1)
    v_ref[0] = (kv[:, D_HEADS_PAD:] + (lane == D_V).astype(F32)).astype(BF16)

    ubuf_ref[0:SUBLANES, :] = halo_ref[b_idx]
    ubuf_ref[SUBLANES:, :] = u_raw
    halo_ref[b_idx] = u_raw[tm - SUBLANES:, :]
    uc = convb_ref[...] + u_raw * convw_ref[CONV_W - 1:CONV_W, :]
    for kk in range(CONV_W - 1):
        off = SUBLANES - (CONV_W - 1) + kk
        uc = uc + ubuf_ref[off:off + tm, :] * convw_ref[kk:kk + 1, :]
    conv_ref[0, 0] = ubuf_ref[tm + SUBLANES - (CONV_W - 1):tm + SUBLANES, :]

    a, bb = _lru_coeffs(uc, wg_ref, ba_ref, bx_ref, lam_ref)

    groups = (tm // SUBLANES, SUBLANES, D_LRU)
    a = a.reshape(groups)
    bb = bb.reshape(groups)
    row8 = lax.broadcasted_iota(jnp.int32, (1, SUBLANES, D_LRU), 1)
    for d in (1, 2, 4):
        a_sh = pltpu.roll(a, d, axis=1)
        b_sh = pltpu.roll(bb, d, axis=1)
        take = row8 >= d
        bb = jnp.where(take, a * b_sh + bb, bb)
        a = jnp.where(take, a * a_sh, a)
    a_sc[...] = a.reshape(tm, D_LRU)
    b_sc[...] = bb.reshape(tm, D_LRU)
    gate_act = _gelu_tanh(gate)

    def group_step(g, hprev):
        off = pl.multiple_of(g * SUBLANES, SUBLANES)
        hg = a_sc[pl.ds(off, SUBLANES), :] * hprev + b_sc[pl.ds(off, SUBLANES), :]
        hs_sc[pl.ds(off, SUBLANES), :] = hg
        return hg[SUBLANES - 1:SUBLANES, :]

    h_fin = lax.fori_loop(0, tm // SUBLANES, group_step, hcar_ref[b_idx], unroll=8)
    hcar_ref[b_idx] = h_fin
    hlast_ref[0, 0] = h_fin

    y = hs_sc[...] * gate_act
    mixl_ref[0] = _rms_norm(y, glru_ref[...], D_LRU).astype(BF16)


def _prompt_pre(apply_ln0, x, mod, ln0_g, ln0_b, lw, tabs):
    bsz, seq, _ = x.shape
    tm = _prompt_tile(seq)
    ns = seq // tm
    cq_tab, sq_tab, ck_tab, sk_tab = tabs
    tile = lambda w: pl.BlockSpec((1, tm, w), lambda s, b: (b, s, 0))
    in_specs = [
        tile(D_MODEL),
        pl.BlockSpec((6, 1, 1, D_MODEL), lambda s, b: (0, b, 0, 0)),
        _const_spec((1, D_MODEL)), _const_spec((1, D_MODEL)),
        _const_spec(lw["w_in"].shape), _const_spec((CONV_W, D_LRU)), _const_spec((1, D_LRU)),
        _const_spec(lw["w_gates"].shape), _const_spec((1, D_LRU)), _const_spec((1, D_LRU)),
        _const_spec((1, D_LRU)), _const_spec((1, Q_RANK)), _const_spec(lw["w_uq"].shape),
        _const_spec((1, KV_RANK)), _const_spec(lw["w_kv"].shape), _const_spec((1, D_LRU)),
        pl.BlockSpec((tm, D_HEADS_PAD), lambda s, b: (s, 0)),
        pl.BlockSpec((tm, D_HEADS_PAD), lambda s, b: (s, 0)),
        pl.BlockSpec((tm, LANES), lambda s, b: (s, 0)),
        pl.BlockSpec((tm, LANES), lambda s, b: (s, 0)),
    ]
    out_specs = [
        tile(D_HEADS_PAD), tile(D_HEADS_PAD), tile(D_HEADS_PAD), tile(D_LRU), tile(KV_RANK),
        tile(D_ROPE),
        pl.BlockSpec((1, 1, 1, D_LRU), lambda s, b: (s, b, 0, 0)),
        pl.BlockSpec((1, 1, CONV_W - 1, D_LRU), lambda s, b: (s, b, 0, 0)),
    ]
    out_shape = [
        jax.ShapeDtypeStruct((bsz, seq, D_HEADS_PAD), BF16),
        jax.ShapeDtypeStruct((bsz, seq, D_HEADS_PAD), BF16),
        jax.ShapeDtypeStruct((bsz, seq, D_HEADS_PAD), BF16),
        jax.ShapeDtypeStruct((bsz, seq, D_LRU), BF16),
        jax.ShapeDtypeStruct((bsz, seq, KV_RANK), F32),
        jax.ShapeDtypeStruct((bsz, seq, D_ROPE), F32),
        jax.ShapeDtypeStruct((ns, bsz, 1, D_LRU), F32),
        jax.ShapeDtypeStruct((ns, bsz, CONV_W - 1, D_LRU), F32),
    ]
    if apply_ln0:
        out_specs = [tile(D_MODEL)] + out_specs
        out_shape = [jax.ShapeDtypeStruct((bsz, seq, D_MODEL), F32)] + out_shape
    return pl.pallas_call(
        functools.partial(_prompt_pre_kernel, apply_ln0, tm),
        grid=(ns, bsz),
        in_specs=in_specs,
        out_specs=out_specs,
        out_shape=out_shape,
        scratch_shapes=[
            pltpu.VMEM((bsz, 1, D_LRU), F32),
            pltpu.VMEM((bsz, SUBLANES, D_LRU), F32),
            pltpu.VMEM((tm + SUBLANES, D_LRU), F32),
            pltpu.VMEM((tm, D_LRU), F32),
            pltpu.VMEM((tm, D_LRU), F32),
            pltpu.VMEM((tm, D_LRU), F32),
        ],
        compiler_params=pltpu.CompilerParams(
            dimension_semantics=("arbitrary", "arbitrary"), vmem_limit_bytes=VMEM_LIMIT),
        name="prompt_pre_ln0" if apply_ln0 else "prompt_pre",
    )(x, mod, ln0_g, ln0_b, lw["w_in"], lw["conv_w"], lw["conv_b"], lw["w_gates"], lw["b_a"],
      lw["b_x"], lw["lam"], lw["q_norm_g"], lw["w_uq"], lw["kv_norm_g"], lw["w_kv"],
      lw["g_lru"], cq_tab, sq_tab, ck_tab, sk_tab)


ATTN_HEADS_PER_STEP = 4


def _attn_kernel(tq, q_ref, k_ref, v_ref, o_ref):
    qi = pl.program_id(2)
    heads = range(ATTN_HEADS_PER_STEP)
    lanes = lambda hh: slice(hh * HEAD_PAD, (hh + 1) * HEAD_PAD)
    qs = [q_ref[0, :, lanes(hh)] for hh in heads]

    def tile_step(j, carry, masked):
        off = pl.multiple_of(j * tq, tq)
        scores, probs, out = {}, {}, {}

        def score(hh):
            kt = k_ref[0, pl.ds(off, tq), lanes(hh)]
            s = lax.dot_general(qs[hh], kt, (((1,), (1,)), ((), ())),
                                preferred_element_type=F32)
            if masked:
                row = lax.broadcasted_iota(jnp.int32, (tq, tq), 0)
                col = lax.broadcasted_iota(jnp.int32, (tq, tq), 1)
                s = jnp.where(col <= row, s, -jnp.inf)
            scores[hh] = s

        def softmax(hh):
            m_prev = carry[hh][0]
            m_new = jnp.maximum(m_prev, jnp.max(scores[hh], axis=1, keepdims=True))
            probs[hh] = (m_new, jnp.exp2(m_prev - m_new),
                         jnp.exp2(scores[hh] - m_new).astype(BF16))

        def weigh(hh):
            m_new, alpha, p = probs[hh]
            vt = v_ref[0, pl.ds(off, tq), lanes(hh)]
            out[hh] = (m_new, alpha * carry[hh][1] + _bdot(p, vt))

        n = len(heads)
        for t in range(n + 2):
            if t < n:
                score(t)
            if 0 <= t - 1 < n:
                softmax(t - 1)
            if 0 <= t - 2 < n:
                weigh(t - 2)
        return tuple(out[hh] for hh in heads)

    init = tuple((jnp.full((tq, 1), -jnp.inf, F32), jnp.zeros((tq, HEAD_PAD), F32))
                 for _ in heads)
    carry = lax.fori_loop(
        0, qi // 2,
        lambda i, c: tile_step(2 * i + 1, tile_step(2 * i, c, False), False), init)

    def finish(final):
        low_half = lax.broadcasted_iota(jnp.int32, (tq, HEAD_PAD), 1) < D_V
        normed = [final[hh][1] / final[hh][1][:, D_V:D_V + 1] for hh in heads]
        for pair in range(ATTN_HEADS_PER_STEP // 2):
            o_ref[0, :, lanes(pair)] = jnp.where(
                low_half, normed[2 * pair], pltpu.roll(normed[2 * pair + 1], D_V, axis=1))

    @pl.when(qi % 2 == 1)
    def _():
        finish(tile_step(qi, tile_step(qi - 1, carry, False), True))

    @pl.when(qi % 2 == 0)
    def _():
        finish(tile_step(qi, carry, True))


def _prompt_attention(q, k, v):
    bsz, seq, _ = q.shape
    tq = _prompt_tile(seq)
    width = ATTN_HEADS_PER_STEP * HEAD_PAD
    out_width = ATTN_HEADS_PER_STEP * D_V
    return pl.pallas_call(
        functools.partial(_attn_kernel, tq),
        grid=(bsz, N_HEADS // ATTN_HEADS_PER_STEP, seq // tq),
        in_specs=[
            pl.BlockSpec((1, tq, width), lambda b, h, i: (b, i, h)),
            pl.BlockSpec((1, seq, width), lambda b, h, i: (b, 0, h)),
            pl.BlockSpec((1, seq, width), lambda b, h, i: (b, 0, h)),
        ],
        out_specs=pl.BlockSpec((1, tq, out_width), lambda b, h, i: (b, i, h)),
        out_shape=jax.ShapeDtypeStruct((bsz, seq, N_HEADS * D_V), F32),
        compiler_params=pltpu.CompilerParams(
            dimension_semantics=("arbitrary", "arbitrary", "arbitrary"),
            vmem_limit_bytes=VMEM_LIMIT),
        name="prompt_attention",
    )(q, k, v)


def _post_kernel(from_latent, ff_chunk, x_ref, mod_ref, mixl_ref, att_ref, wuv_ref, gatt_ref,
                 wo_ref, lng_ref, lnb_ref, wup_ref, wdn_ref, o_ref):
    tm = x_ref.shape[1]
    n_half = 2 if tm % 32 == 0 else 1
    hr = tm // n_half
    rows = [slice(i * hr, (i + 1) * hr) for i in range(n_half)]
    per_row = mod_ref.shape[2] != 1

    def mod(k, r):
        return mod_ref[k, 0][r] if per_row else mod_ref[k, 0]

    def project(r):
        if from_latent:
            lat = jnp.concatenate([att_ref[hh, r, :] for hh in range(N_HEADS)], axis=1)
            ya = _bdot(lat, wuv_ref[...])
        else:
            ya = att_ref[0, r, :]
        na = _rms_norm(ya, gatt_ref[...], N_HEADS * D_V).astype(BF16)
        return _bdot(jnp.concatenate([mixl_ref[0, r, :], na], axis=1), wo_ref[...])

    def norm1(r, o):
        x1 = _layer_norm(ALPHA * x_ref[0, r, :] + mod(2, r) * o, lng_ref[0:1, :], lnb_ref[0:1, :])
        return x1, (x1 * (1.0 + mod(4, r)) + mod(3, r)).astype(BF16)

    def mlp(h2):
        f = jnp.zeros((hr, D_MODEL), F32)
        for c in range(D_FF // ff_chunk):
            up = _bdot(h2, wup_ref[:, c * ff_chunk:(c + 1) * ff_chunk])
            act = jnp.square(jnp.maximum(up, 0.0)).astype(BF16)
            f = f + _bdot(act, wdn_ref[c * ff_chunk:(c + 1) * ff_chunk, :])
        return f

    def norm2(r, x1, f):
        o_ref[0, r, :] = _layer_norm(ALPHA * x1 + mod(5, r) * f, lng_ref[1:2, :], lnb_ref[1:2, :])

    outs = [project(r) for r in rows]
    mids = {}
    mids[0] = norm1(rows[0], outs[0])
    ffs = {0: mlp(mids[0][1])}
    for i in range(1, n_half):
        mids[i] = norm1(rows[i], outs[i])
        ffs[i] = mlp(mids[i][1])
        norm2(rows[i - 1], mids[i - 1][0], ffs[i - 1])
    norm2(rows[n_half - 1], mids[n_half - 1][0], ffs[n_half - 1])


def _post(from_latent, x, mod, mixl, att, lw):
    groups, rows, _ = x.shape
    tm = _prompt_tile(rows)
    tile = lambda w: pl.BlockSpec((1, tm, w), lambda g, s: (g, s, 0))
    mod_rows = mod.shape[2]
    if from_latent:
        att_spec = pl.BlockSpec((N_HEADS, tm, KV_RANK), lambda g, s: (0, s, 0))
    else:
        att_spec = tile(N_HEADS * D_V)
    return pl.pallas_call(
        functools.partial(_post_kernel, from_latent, 1024),
        grid=(groups, rows // tm),
        in_specs=[
            tile(D_MODEL),
            pl.BlockSpec((6, 1, mod_rows, D_MODEL), lambda g, s: (0, g, 0, 0)),
            tile(D_LRU), att_spec,
            _const_spec(lw["w_uv_heads"].shape), _const_spec((1, N_HEADS * D_V)),
            _const_spec(lw["w_o"].shape),
            _const_spec((2, D_MODEL)), _const_spec((2, D_MODEL)),
            _const_spec(lw["w_up"].shape), _const_spec(lw["w_down"].shape),
        ],
        out_specs=tile(D_MODEL),
        out_shape=jax.ShapeDtypeStruct(x.shape, F32),
        compiler_params=pltpu.CompilerParams(
            dimension_semantics=("arbitrary", "arbitrary"), vmem_limit_bytes=VMEM_LIMIT),
        name="sample_post" if from_latent else "prompt_post",
    )(x, mod, mixl, att, lw["w_uv_heads"], lw["g_att"], lw["w_o"],
      lw["ln_g"], lw["ln_b"], lw["w_up"], lw["w_down"])


def _sample_pre_kernel(apply_ln0, *refs):
    (x_ref, mod_ref, ln0g_ref, ln0b_ref, win_ref, convw_ref, convb_ref, wg_ref, ba_ref,
     bx_ref, lam_ref, qg_ref, wuq_ref, kvg_ref, wukt_ref, glru_ref,
     cq_ref, sq_ref, ck_ref, sk_ref, h0_ref, buf_ref) = refs[:22]
    outs = refs[22:]
    if apply_ln0:
        xn_ref, outs = outs[0], outs[1:]
    q_ref, qlat_ref, mixl_ref, ckv_ref, kr_ref, hnew_ref, conv_ref = outs

    x = x_ref[...]
    if apply_ln0:
        x = _layer_norm(x, ln0g_ref[...], ln0b_ref[...])
        xn_ref[...] = x
    h = (x * (1.0 + mod_ref[1, 0]) + mod_ref[0, 0]).astype(BF16)
    proj = _bdot(h, win_ref[...])
    u_raw = proj[:, OFF_U:OFF_GATE]
    gate = proj[:, OFF_GATE:OFF_CQ]

    uc = convb_ref[...] + u_raw * convw_ref[CONV_W - 1:CONV_W, :]
    for kk in range(CONV_W - 1):
        uc = uc + buf_ref[kk] * convw_ref[kk:kk + 1, :]
    for kk in range(CONV_W - 2):
        conv_ref[kk] = buf_ref[kk + 1]
    conv_ref[CONV_W - 2] = u_raw

    a, bb = _lru_coeffs(uc, wg_ref, ba_ref, bx_ref, lam_ref)
    h_new = a * h0_ref[...] + bb
    hnew_ref[...] = h_new
    y = h_new * _gelu_tanh(gate)
    mixl_ref[...] = _rms_norm(y, glru_ref[...], D_LRU).astype(BF16)

    q, ckvn, krw = _latent_heads(proj[:, OFF_CQ:], cq_ref[0:1, :], sq_ref[0:1, :],
                                 ck_ref[0:1, :], sk_ref[0:1, :], qg_ref, wuq_ref, kvg_ref)
    qb = q.astype(BF16)
    q_ref[...] = qb
    ckv_ref[...] = ckvn
    kr_ref[...] = krw[:, OFF_KR_LANE:OFF_KR_LANE + D_ROPE]
    for hh in range(N_HEADS):
        qlat_ref[hh] = _bdot(qb[:, hh * HEAD_PAD:(hh + 1) * HEAD_PAD], wukt_ref[hh])


def _sample_pre(apply_ln0, x, mod, ln0_g, ln0_b, lw, tabs, h0, buf):
    rows = x.shape[0]
    cq_tab, sq_tab, ck_tab, sk_tab = tabs
    full = lambda a: _const_spec(a.shape)
    args = (x, mod, ln0_g, ln0_b, lw["w_in"], lw["conv_w"], lw["conv_b"], lw["w_gates"],
            lw["b_a"], lw["b_x"], lw["lam"], lw["q_norm_g"], lw["w_uq"], lw["kv_norm_g"],
            lw["w_uk_t"], lw["g_lru"], cq_tab, sq_tab, ck_tab, sk_tab, h0, buf)
    out_shape = [
        jax.ShapeDtypeStruct((rows, D_HEADS_PAD), BF16),
        jax.ShapeDtypeStruct((N_HEADS, rows, KV_RANK), F32),
        jax.ShapeDtypeStruct((rows, D_LRU), BF16),
        jax.ShapeDtypeStruct((rows, KV_RANK), F32),
        jax.ShapeDtypeStruct((rows, D_ROPE), F32),
        jax.ShapeDtypeStruct((rows, D_LRU), F32),
        jax.ShapeDtypeStruct((CONV_W - 1, rows, D_LRU), F32),
    ]
    if apply_ln0:
        out_shape = [jax.ShapeDtypeStruct((rows, D_MODEL), F32)] + out_shape
    return pl.pallas_call(
        functools.partial(_sample_pre_kernel, apply_ln0),
        grid=(1,),
        in_specs=[full(a) for a in args],
        out_specs=[_const_spec(s.shape) for s in out_shape],
        out_shape=out_shape,
        compiler_params=pltpu.CompilerParams(
            dimension_semantics=("arbitrary",), vmem_limit_bytes=VMEM_LIMIT),
        name="sample_pre_ln0" if apply_ln0 else "sample_pre",
    )(*args)


PAGED_SLOTS = 4


def _paged_attn_kernel(layer, pages_per_chunk, chunks_per_seq, page_size,
                       pt_ref, qlat_ref, qrope_ref, cnew_ref, knew_ref, cache_c, cache_kt,
                       o_ref, *scratch):
    cbufs = scratch[:PAGED_SLOTS]
    kbufs = scratch[PAGED_SLOTS:2 * PAGED_SLOTS]
    sem_c, sem_k, m_sc, l_sc, acc_sc = scratch[2 * PAGED_SLOTS:]
    g = pl.program_id(0)
    n_steps = pl.num_programs(0)
    total_chunks = PAGED_SLOTS * n_steps
    bufs = tuple(zip(cbufs, kbufs))
    ahead = PAGED_SLOTS - 1

    def page_copies(chunk_idx, sl, j):
        seq = chunk_idx // chunks_per_seq
        page = pt_ref[seq, (chunk_idx % chunks_per_seq) * pages_per_chunk + j]
        span = pl.ds(j * page_size, page_size)
        cb, kb = bufs[sl]
        return (pltpu.make_async_copy(cache_c.at[layer, page], cb.at[span, :], sem_c.at[sl]),
                pltpu.make_async_copy(cache_kt.at[layer, page], kb.at[:, span], sem_k.at[sl]))

    def start_chunk(chunk_idx, sl):
        for j in range(pages_per_chunk):
            for cp in page_copies(chunk_idx, sl, j):
                cp.start()

    def wait_chunk(chunk_idx, sl):
        for j in range(pages_per_chunk):
            for cp in page_copies(chunk_idx, sl, j):
                cp.wait()

    qlat = qlat_ref[0]
    qrope = qrope_ref[0]

    def consume(sl, m_prev, l_prev, acc_prev):
        cb, kb = bufs[sl]
        ck = cb[...].astype(BF16)
        s = (lax.dot_general(qlat, ck, (((1,), (1,)), ((), ())), preferred_element_type=F32)
             + _bdot(qrope, kb[...].astype(BF16)))
        m_new = jnp.maximum(m_prev, jnp.max(s, axis=1, keepdims=True))
        alpha = jnp.exp2(m_prev - m_new)
        p = jnp.exp2(s - m_new)
        l_new = alpha * l_prev + jnp.sum(p, axis=1, keepdims=True)
        acc = alpha * acc_prev + _bdot(p.astype(BF16), ck)
        return m_new, l_new, acc

    first = PAGED_SLOTS * g

    @pl.when(g == 0)
    def _():
        for k in range(ahead):
            start_chunk(k, k)

    cnew = cnew_ref[0].astype(BF16).astype(F32)
    knew = knew_ref[0].astype(BF16).astype(F32)
    s_new = (jnp.sum(qlat.astype(F32) * cnew, axis=1, keepdims=True)
             + jnp.sum(qrope.astype(F32) * knew, axis=1, keepdims=True))
    opens = lax.rem(first, chunks_per_seq) == 0
    state = (jnp.where(opens, s_new, m_sc[...]),
             jnp.where(opens, jnp.ones_like(s_new), l_sc[...]),
             jnp.where(opens, jnp.broadcast_to(cnew, acc_sc.shape), acc_sc[...]))

    for k in range(PAGED_SLOTS):
        wait_chunk(first + k, k)
        state = consume(k, *state)
        start_chunk(lax.rem(first + k + ahead, total_chunks), (k + ahead) % PAGED_SLOTS)

    m_fin, l_fin, acc_fin = state
    m_sc[...] = m_fin
    l_sc[...] = l_fin
    acc_sc[...] = acc_fin
    o_ref[0] = acc_fin / l_fin

    @pl.when(g == n_steps - 1)
    def _():
        for k in range(ahead):
            wait_chunk(k, k)


def _paged_pages_per_chunk(n_pages):
    for p in (32, 16, 8, 4, 2, 1):
        if n_pages % (PAGED_SLOTS * p) == 0:
            return p
    raise ValueError(f"the page count per sequence ({n_pages}) must be a multiple of {PAGED_SLOTS}")


def _paged_attention(layer, page_table, qlat, qrope, cnew, knew, cache_ckv, cache_krope_t):
    nseq, n_pages = page_table.shape
    page_size = cache_ckv.shape[2]
    ppc = _paged_pages_per_chunk(n_pages)
    chunks_per_seq = n_pages // ppc
    steps_per_seq = chunks_per_seq // PAGED_SLOTS
    per_seq = lambda w: pl.BlockSpec((1, N_HEADS, w), lambda s, pt: (s // steps_per_seq, 0, 0))
    one_row = lambda w: pl.BlockSpec((1, 1, w), lambda s, pt: (s // steps_per_seq, 0, 0))
    grid_spec = pltpu.PrefetchScalarGridSpec(
        num_scalar_prefetch=1,
        grid=(nseq * steps_per_seq,),
        in_specs=[
            per_seq(KV_RANK), per_seq(D_ROPE), one_row(KV_RANK), one_row(D_ROPE),
            pl.BlockSpec(memory_space=pl.ANY), pl.BlockSpec(memory_space=pl.ANY),
        ],
        out_specs=per_seq(KV_RANK),
        scratch_shapes=(
            [pltpu.VMEM((ppc * page_size, KV_RANK), F32)] * PAGED_SLOTS
            + [pltpu.VMEM((D_ROPE, ppc * page_size), F32)] * PAGED_SLOTS
        ) + [
            pltpu.SemaphoreType.DMA((PAGED_SLOTS,)),
            pltpu.SemaphoreType.DMA((PAGED_SLOTS,)),
            pltpu.VMEM((N_HEADS, 1), F32),
            pltpu.VMEM((N_HEADS, 1), F32),
            pltpu.VMEM((N_HEADS, KV_RANK), F32),
        ],
    )
    return pl.pallas_call(
        functools.partial(_paged_attn_kernel, layer, ppc, chunks_per_seq, page_size),
        grid_spec=grid_spec,
        out_shape=jax.ShapeDtypeStruct((nseq, N_HEADS, KV_RANK), F32),
        compiler_params=pltpu.CompilerParams(
            dimension_semantics=("arbitrary",), vmem_limit_bytes=VMEM_LIMIT),
        name="paged_attention",
    )(page_table, qlat, qrope, cnew, knew, cache_ckv, cache_krope_t)


def _pad_heads(w, width):
    pad = [(0, 0)] * (w.ndim - 1) + [(0, HEAD_PAD - width)]
    return jnp.pad(w, pad).reshape(*w.shape[:-2], D_HEADS_PAD)


def _block_diag_chunks(w):
    per = GATE_CHUNK // LRU_BLOCK
    eye = jnp.eye(per, dtype=w.dtype)
    w = w.reshape(D_LRU // GATE_CHUNK, per, LRU_BLOCK, LRU_BLOCK)
    return jnp.einsum("cnij,nm->cnimj", w, eye).reshape(-1, GATE_CHUNK, GATE_CHUNK)


def _layer_weights(l, p):
    w_in = p["w_in"][l]
    kr_cols = w_in[:, OFF_KRA:OFF_KRA + D_ROPE]
    kr_swap = jnp.concatenate([-kr_cols[:, HALF_ROPE:], kr_cols[:, :HALF_ROPE]], axis=1)
    lane_pad = lambda w: jnp.pad(w, ((0, 0), (OFF_KR_LANE, LANES - OFF_KR_LANE - w.shape[1])))
    w_in_pad = jnp.concatenate([w_in[:, :OFF_KRA], lane_pad(kr_cols), lane_pad(kr_swap)], axis=1)

    w_uq = p["w_uq"][l]
    x1 = w_uq[..., D_NOPE:D_NOPE + HALF_ROPE]
    x2 = w_uq[..., D_NOPE + HALF_ROPE:]
    w_uq_swap = jnp.concatenate([jnp.zeros_like(w_uq[..., :D_NOPE]), -x2, x1], axis=-1)
    w_uq_pad = jnp.concatenate([_pad_heads(w_uq, D_QK), _pad_heads(w_uq_swap, D_QK)], axis=1)

    w_uk = p["w_uk"][l]
    w_uv = p["w_uv"][l]
    w_kv = jnp.concatenate([_pad_heads(w_uk, D_NOPE), _pad_heads(w_uv, D_V)], axis=1)

    w_uk_t = jnp.pad(jnp.transpose(w_uk, (1, 2, 0)), ((0, 0), (0, HEAD_PAD - D_NOPE), (0, 0)))
    w_uv_heads = jnp.einsum("rhv,hg->hrgv", w_uv, jnp.eye(N_HEADS, dtype=w_uv.dtype)).reshape(
        N_HEADS * KV_RANK, N_HEADS * D_V)
    row = lambda v: v.reshape(1, -1)
    return {
        "w_in": w_in_pad.astype(BF16),
        "conv_w": p["conv_w"][l], "conv_b": row(p["conv_b"][l]),
        "w_gates": jnp.stack([_block_diag_chunks(p["w_a"][l]),
                              _block_diag_chunks(p["w_x"][l])]).astype(BF16),
        "b_a": row(p["b_a"][l]), "b_x": row(p["b_x"][l]), "lam": row(p["lru_lambda"][l]),
        "q_norm_g": row(p["q_norm_g"][l]), "w_uq": w_uq_pad.astype(BF16),
        "kv_norm_g": row(p["kv_norm_g"][l]), "w_kv": w_kv.astype(BF16),
        "w_uk_t": w_uk_t.astype(BF16), "w_uv_heads": w_uv_heads.astype(BF16),
        "g_lru": row(p["g_lru"][l]), "g_att": row(p["g_att"][l]),
        "w_o": p["w_o"][l].astype(BF16),
        "ln_g": p["ln_g"][l], "ln_b": p["ln_b"][l],
        "w_up": p["w_up"][l].astype(BF16), "w_down": p["w_down"][l].astype(BF16),
    }


def kernel(x_prompt, x_sample, cache_ckv, cache_krope, state_lru_h, state_conv, page_table,
           c_prompt, c_sample, ln0_g, ln0_b, w_ada, b_ada, w_in, conv_w, conv_b, w_a, b_a,
           w_x, b_x, lru_lambda, q_norm_g, w_uq, kv_norm_g, w_uk, w_uv, g_lru, g_att, w_o,
           ln_g, ln_b, w_up, w_down):
    params = dict(w_in=w_in, conv_w=conv_w, conv_b=conv_b, w_a=w_a, b_a=b_a, w_x=w_x, b_x=b_x,
                  lru_lambda=lru_lambda, q_norm_g=q_norm_g, w_uq=w_uq, kv_norm_g=kv_norm_g,
                  w_uk=w_uk, w_uv=w_uv, g_lru=g_lru, g_att=g_att, w_o=w_o, ln_g=ln_g,
                  ln_b=ln_b, w_up=w_up, w_down=w_down)
    b_p, s_p, _ = x_prompt.shape
    b_s, s_s, _ = x_sample.shape
    if s_s != 1:
        raise ValueError("the sample group is a single-token decode step")
    depth = w_ada.shape[0]
    past_len = page_table.shape[1] * cache_ckv.shape[2]

    mods = _ada_mod(jnp.concatenate([c_prompt, c_sample], axis=0), w_ada, b_ada)
    tabs_p = _rope_tables(s_p, 0)
    tabs_s = _rope_tables(SUBLANES, past_len)
    ln0g, ln0b = ln0_g.reshape(1, -1), ln0_b.reshape(1, -1)
    cache_krope_t = jnp.swapaxes(cache_krope, 2, 3)

    xp = x_prompt
    xs = x_sample.reshape(b_s, D_MODEL)
    outs_p = [[] for _ in range(4)]
    outs_s = [[] for _ in range(4)]
    for l in range(depth):
        lw = _layer_weights(l, params)
        first = l == 0
        mod_p = mods[l, :, :b_p].reshape(6, b_p, 1, D_MODEL)
        mod_s = mods[l, :, b_p:].reshape(6, 1, b_s, D_MODEL)

        res = _prompt_pre(first, xp, mod_p, ln0g, ln0b, lw, tabs_p)
        if first:
            xp, res = res[0], res[1:]
        q, k, v, mixl, ckv, kr, h_last, conv = res
        att = _prompt_attention(q, k, v)
        xp = _post(False, xp, mod_p, mixl, att, lw)
        for dst, val in zip(outs_p, (ckv, kr, h_last[-1].reshape(b_p, D_LRU), conv[-1])):
            dst.append(val)

        res = _sample_pre(first, xs, mod_s, ln0g, ln0b, lw, tabs_s, state_lru_h[l],
                          jnp.transpose(state_conv[l], (1, 0, 2)))
        if first:
            xs, res = res[0], res[1:]
        q_s, qlat, mixl_s, ckv_s, kr_s, h_new, conv_s = res
        qlat_b = jnp.transpose(qlat, (1, 0, 2)).astype(BF16)
        qrope_b = q_s.reshape(b_s, N_HEADS, HEAD_PAD)[:, :, D_NOPE:D_QK]
        o_lat = _paged_attention(l, page_table, qlat_b, qrope_b, ckv_s.reshape(b_s, 1, KV_RANK),
                                 kr_s.reshape(b_s, 1, D_ROPE), cache_ckv, cache_krope_t)
        o_lat_h = jnp.transpose(o_lat, (1, 0, 2)).astype(BF16)
        xs = _post(True, xs.reshape(1, b_s, D_MODEL), mod_s, mixl_s.reshape(1, b_s, D_LRU),
                   o_lat_h, lw).reshape(b_s, D_MODEL)
        for dst, val in zip(outs_s, (ckv_s.reshape(b_s, 1, KV_RANK), kr_s.reshape(b_s, 1, D_ROPE),
                                     h_new, jnp.transpose(conv_s, (1, 0, 2)))):
            dst.append(val)

    stack = lambda vals: jnp.stack(vals, axis=0)
    return (xp, xs.reshape(b_s, 1, D_MODEL),
            stack(outs_p[0]), stack(outs_p[1]), stack(outs_p[2]), stack(outs_p[3]),
            stack(outs_s[0]), stack(outs_s[1]), stack(outs_s[2]), stack(outs_s[3]))
```
